```python
import math, functools
import jax, jax.numpy as jnp
from jax import lax
import numpy as np

D_MODEL = 2048
BATCH = 4
SEQ = 2048
DEPTH = 1
DEC_BATCH = 32
DEC_SEQ = 4
PAST_LEN = 8192
PAGE_SIZE = 128

MIX_W = D_MODEL
ATT_W = MIX_W // 2
CONV_W = MIX_W - ATT_W
ATT_HEADS = 8
V_DIM = ATT_W // ATT_HEADS
QK_DIM = V_DIM // 2
ROPE_DIM = QK_DIM // 4
ROPE_THETA = 500000.0
CONV_K = 31
D_FF = 4 * D_MODEL
Q_BLOCK = 128
EPS = 1e-6
IN_W = 3 * ATT_W + 2 * CONV_W

kernel_name = "hymba_diffattn_conformer_decoder_step"


def rms_norm(x, g):
    xf = x.astype(jnp.float32)
    y = xf * lax.rsqrt(jnp.mean(xf * xf, axis=-1, keepdims=True) + EPS)
    return (y * g.astype(jnp.float32)).astype(x.dtype)


def layer_norm(x, g, b):
    xf = x.astype(jnp.float32)
    mu = jnp.mean(xf, axis=-1, keepdims=True)
    xc = xf - mu
    y = xc * lax.rsqrt(jnp.mean(xc * xc, axis=-1, keepdims=True) + EPS)
    return (y * g.astype(jnp.float32) + b.astype(jnp.float32)).astype(x.dtype)


def rope(x, pos):
    half = ROPE_DIM // 2
    inv = jnp.power(ROPE_THETA, -jnp.arange(half, dtype=jnp.float32) * 2.0 / ROPE_DIM)
    ang = pos.astype(jnp.float32)[:, None] * inv[None, :]
    cos = jnp.cos(ang)[:, None, None, :]
    sin = jnp.sin(ang)[:, None, None, :]
    xf = x.astype(jnp.float32)
    x1 = xf[..., :half]
    x2 = xf[..., half:ROPE_DIM]
    out = jnp.concatenate([x1 * cos - x2 * sin, x2 * cos + x1 * sin, xf[..., ROPE_DIM:]], axis=-1)
    return out.astype(x.dtype)


def diff_scores(q, k):
    return jnp.einsum('bqhmd,bkhmd->bhmqk', q, k).astype(jnp.float32) * (QK_DIM ** -0.5)


def diff_weights(s, mask, lam):
    s = jnp.where(mask, s, jnp.finfo(jnp.float32).min)
    p = jax.nn.softmax(s, axis=-1)
    return p[:, :, 0] - lam * p[:, :, 1]


def prompt_attention(q, k, v, pos, lam):
    B, T = q.shape[0], q.shape[1]
    nb = T // Q_BLOCK
    qb = q.reshape(B, nb, Q_BLOCK, ATT_HEADS, 2, QK_DIM).swapaxes(0, 1)
    starts = jnp.arange(nb, dtype=jnp.int32) * Q_BLOCK

    def one_block(args):
        qi, st = args
        q_pos = st + jnp.arange(Q_BLOCK, dtype=jnp.int32)
        mask = pos[None, :] <= q_pos[:, None]
        a = diff_weights(diff_scores(qi, k), mask, lam)
        return jnp.einsum('bhqk,bkhd->bqhd', a.astype(v.dtype), v)

    o = lax.map(one_block, (qb, starts))
    return o.swapaxes(0, 1).reshape(B, T, ATT_HEADS, V_DIM)


def sample_attention(q, k, v, pos, lam, cache_k, cache_v, page_table, layer):
    B, T = q.shape[0], q.shape[1]
    pk = cache_k[layer, page_table]
    pv = cache_v[layer, page_table]
    past = pk.shape[1] * pk.shape[2]
    pk = pk.reshape(B, past, ATT_HEADS, 2, QK_DIM).astype(k.dtype)
    pv = pv.reshape(B, past, ATT_HEADS, V_DIM).astype(v.dtype)
    s = jnp.concatenate([diff_scores(q, pk), diff_scores(q, k)], axis=-1)
    new_mask = jnp.arange(T)[None, :] <= jnp.arange(T)[:, None]
    mask = jnp.concatenate([jnp.ones((T, past), dtype=bool), new_mask], axis=-1)
    a = diff_weights(s, mask, lam).astype(v.dtype)
    return (jnp.einsum('bhqk,bkhd->bqhd', a[..., :past], pv)
            + jnp.einsum('bhqk,bkhd->bqhd', a[..., past:], v))


def conv_module(g, prefix, conv_w, conv_b, ln_g, ln_b):
    u = g[..., :CONV_W] * jax.nn.sigmoid(g[..., CONV_W:])
    ext = jnp.concatenate([prefix.astype(u.dtype), u], axis=1)
    y = lax.conv_general_dilated(ext, conv_w[:, None, :].astype(ext.dtype), (1,), 'VALID',
                                 dimension_numbers=('NWC', 'WIO', 'NWC'),
                                 feature_group_count=CONV_W) + conv_b
    y = layer_norm(y, ln_g, ln_b)
    y = y * jax.nn.sigmoid(y)
    return y, ext[:, -(CONV_K - 1):]


def hybrid_layer(x, pos, attend_fn, conv_prefix, lam, lam_init, norm1_g, w_in, subln_g,
                 conv_w, conv_b, conv_ln_g, conv_ln_b, w_out, norm2_g, w_up, w_down):
    B, T, _ = x.shape
    xn = rms_norm(x, norm1_g)
    z = xn @ w_in
    q, k, v, g = jnp.split(z, [ATT_W, 2 * ATT_W, 3 * ATT_W], axis=-1)
    q = rope(q.reshape(B, T, ATT_HEADS, 2, QK_DIM), pos)
    k = rope(k.reshape(B, T, ATT_HEADS, 2, QK_DIM), pos)
    v = v.reshape(B, T, ATT_HEADS, V_DIM)
    o = attend_fn(q, k, v, pos, lam)
    o = rms_norm(o, subln_g) * (1.0 - lam_init)
    c, conv_state = conv_module(g, conv_prefix, conv_w, conv_b, conv_ln_g, conv_ln_b)
    h = x + jnp.concatenate([o.reshape(B, T, ATT_W), c], axis=-1) @ w_out
    hn = rms_norm(h, norm2_g)
    y = h + jnp.square(jax.nn.relu(hn @ w_up)) @ w_down
    return y, k.reshape(B, T, ATT_HEADS, 2 * QK_DIM), v, conv_state


def setup_inputs(seed: int = 0) -> dict:
    key = jax.random.key(seed)
    ks = jax.random.split(key, 24)
    f32 = jnp.float32
    n_pages = PAST_LEN // PAGE_SIZE
    used = DEC_BATCH * n_pages
    n_pool = used + max(1, used // 4)
    nrm = lambda k, shape, s: jax.random.normal(k, shape, f32) * s
    page_table = jax.random.permutation(ks[5], n_pool)[:used].reshape(DEC_BATCH, n_pages).astype(jnp.int32)
    return {
        'x_prompt': nrm(ks[0], (BATCH, SEQ, D_MODEL), 1.0),
        'x_sample': nrm(ks[1], (DEC_BATCH, DEC_SEQ, D_MODEL), 1.0),
        'cache_k': nrm(ks[2], (DEPTH, n_pool, PAGE_SIZE, ATT_HEADS, 2 * QK_DIM), 1.0),
        'cache_v': nrm(ks[3], (DEPTH, n_pool, PAGE_SIZE, ATT_HEADS, V_DIM), 1.0),
        'state_conv': nrm(ks[4], (DEPTH, DEC_BATCH, CONV_K - 1, CONV_W), 0.5),
        'page_table': page_table,
        'norm1_g': 1.0 + nrm(ks[6], (DEPTH, D_MODEL), 0.02),
        'w_in': nrm(ks[7], (DEPTH, D_MODEL, IN_W), D_MODEL ** -0.5),
        'lambda_q1': nrm(ks[8], (DEPTH, QK_DIM), 0.1),
        'lambda_k1': nrm(ks[9], (DEPTH, QK_DIM), 0.1),
        'lambda_q2': nrm(ks[10], (DEPTH, QK_DIM), 0.1),
        'lambda_k2': nrm(ks[11], (DEPTH, QK_DIM), 0.1),
        'subln_g': 1.0 + nrm(ks[12], (DEPTH, V_DIM), 0.02),
        'conv_w': nrm(ks[13], (DEPTH, CONV_K, CONV_W), CONV_K ** -0.5),
        'conv_b': nrm(ks[14], (DEPTH, CONV_W), 0.02),
        'conv_ln_g': 1.0 + nrm(ks[15], (DEPTH, CONV_W), 0.02),
        'conv_ln_b': nrm(ks[16], (DEPTH, CONV_W), 0.02),
        'w_out': nrm(ks[17], (DEPTH, MIX_W, D_MODEL), MIX_W ** -0.5),
        'norm2_g': 1.0 + nrm(ks[18], (DEPTH, D_MODEL), 0.02),
        'w_up': nrm(ks[19], (DEPTH, D_MODEL, D_FF), D_MODEL ** -0.5),
        'w_down': nrm(ks[20], (DEPTH, D_FF, D_MODEL), D_FF ** -0.5),
        'final_g': 1.0 + nrm(ks[21], (D_MODEL,), 0.02),
    }


def reference(x_prompt, x_sample, cache_k, cache_v, state_conv, page_table, norm1_g, w_in,
              lambda_q1, lambda_k1, lambda_q2, lambda_k2, subln_g, conv_w, conv_b,
              conv_ln_g, conv_ln_b, w_out, norm2_g, w_up, w_down, final_g):
    past = page_table.shape[1] * cache_k.shape[2]
    pos_p = jnp.arange(x_prompt.shape[1], dtype=jnp.int32)
    pos_s = past + jnp.arange(x_sample.shape[1], dtype=jnp.int32)
    xp, xs = x_prompt, x_sample
    kp_l, vp_l, cp_l, ks_l, vs_l, cs_l = [], [], [], [], [], []
    for l in range(DEPTH):
        lam_init = 0.8 - 0.6 * math.exp(-0.3 * l)
        lam = (jnp.exp(jnp.sum(lambda_q1[l].astype(jnp.float32) * lambda_k1[l].astype(jnp.float32)))
               - jnp.exp(jnp.sum(lambda_q2[l].astype(jnp.float32) * lambda_k2[l].astype(jnp.float32)))
               + lam_init)
        wl = (norm1_g[l], w_in[l], subln_g[l], conv_w[l], conv_b[l], conv_ln_g[l], conv_ln_b[l],
              w_out[l], norm2_g[l], w_up[l], w_down[l])
        zero_prefix = jnp.zeros((xp.shape[0], CONV_K - 1, CONV_W), xp.dtype)
        xp, kp, vp, cp = hybrid_layer(xp, pos_p, prompt_attention, zero_prefix, lam, lam_init, *wl)
        s_attn = functools.partial(sample_attention, cache_k=cache_k, cache_v=cache_v,
                                   page_table=page_table, layer=l)
        xs, kn, vn, cn = hybrid_layer(xs, pos_s, s_attn, state_conv[l], lam, lam_init, *wl)
        kp_l.append(kp); vp_l.append(vp); cp_l.append(cp)
        ks_l.append(kn); vs_l.append(vn); cs_l.append(cn)
    y_prompt = rms_norm(xp, final_g)
    y_sample = rms_norm(xs, final_g)
    return (y_prompt, y_sample, jnp.stack(kp_l), jnp.stack(vp_l), jnp.stack(cp_l),
            jnp.stack(ks_l), jnp.stack(vs_l), jnp.stack(cs_l))
```

```python
import functools
import math

import jax
import jax.numpy as jnp
from jax import lax
from jax.experimental import pallas as pl
from jax.experimental.pallas import tpu as pltpu

F32 = jnp.float32
BF16 = jnp.bfloat16

EPS = 1e-6
ROPE_THETA = 500000.0
LANES = 128
HEAD_W = 128
QK_DIM = 64
ROPE_DIM = 16
CONV_K = 31
NEG = -1e30
VMEM_LIMIT = 56 * 1024 * 1024


def _cparams(sem):
    return pltpu.CompilerParams(dimension_semantics=sem, vmem_limit_bytes=VMEM_LIMIT)


def _rope(z, c, s1, s2):
    return z * c + pltpu.roll(z, LANES - 8, 1) * s1 + pltpu.roll(z, 8, 1) * s2


def _inproj_kernel(x_ref, g_ref, w_ref, c_ref, s1_ref, s2_ref,
                   q_ref, k_ref, kb_ref, v_ref, vb_ref, u_ref, xn_ref, gv_ref):
    j = pl.program_id(1)

    @pl.when(j == 0)
    def _():
        x = x_ref[...]
        ms = jnp.mean(x * x, axis=-1, keepdims=True)
        xn_ref[...] = (x * lax.rsqrt(ms + EPS) * g_ref[...]).astype(BF16)

    z = jnp.dot(xn_ref[...], w_ref[...], preferred_element_type=F32)
    n_heads = z.shape[1] // HEAD_W

    @pl.when(j == 0)
    def _():
        c, s1, s2 = c_ref[...], s1_ref[...], s2_ref[...]
        for h in range(n_heads):
            sl = slice(h * HEAD_W, (h + 1) * HEAD_W)
            q_ref[:, sl] = (_rope(z[:, sl], c, s1, s2) * (QK_DIM ** -0.5)).astype(BF16)

    @pl.when(j == 1)
    def _():
        c, s1, s2 = c_ref[...], s1_ref[...], s2_ref[...]
        for h in range(n_heads):
            sl = slice(h * HEAD_W, (h + 1) * HEAD_W)
            kr = _rope(z[:, sl], c, s1, s2)
            k_ref[:, sl] = kr
            kb_ref[:, sl] = kr.astype(BF16)

    @pl.when(j == 2)
    def _():
        v_ref[...] = z
        vb_ref[...] = z.astype(BF16)

    @pl.when(j == 3)
    def _():
        gv_ref[...] = z

    @pl.when(j == 4)
    def _():
        u_ref[...] = gv_ref[...] * (1.0 / (1.0 + jnp.exp(-z)))


def _inproj(x, g, w_bf, tabs, tm, n_pos_blocks):
    m, d = x.shape
    nw = w_bf.shape[1] // 5
    row = lambda i, j: (i, 0)
    tab = lambda i, j: (i % n_pos_blocks, 0)
    out_f = jax.ShapeDtypeStruct((m, nw), F32)
    out_b = jax.ShapeDtypeStruct((m, nw), BF16)
    return pl.pallas_call(
        _inproj_kernel,
        grid=(m // tm, 5),
        in_specs=[pl.BlockSpec((tm, d), row),
                  pl.BlockSpec((1, d), lambda i, j: (0, 0)),
                  pl.BlockSpec((d, nw), lambda i, j: (0, j)),
                  pl.BlockSpec((tm, LANES), tab),
                  pl.BlockSpec((tm, LANES), tab),
                  pl.BlockSpec((tm, LANES), tab)],
        out_specs=[pl.BlockSpec((tm, nw), row)] * 6,
        out_shape=[out_b, out_f, out_b, out_f, out_b, out_f],
        scratch_shapes=[pltpu.VMEM((tm, d), BF16), pltpu.VMEM((tm, nw), F32)],
        compiler_params=_cparams(("parallel", "arbitrary")),
        name="inproj",
    )(x, g, w_bf, *tabs)


def _rope_tables(pos):
    half = ROPE_DIM // 2
    inv = jnp.power(ROPE_THETA, -jnp.arange(half, dtype=F32) * 2.0 / ROPE_DIM)
    ang = pos.astype(F32)[:, None] * inv[None, :]
    cos, sin = jnp.cos(ang), jnp.sin(ang)
    t = pos.shape[0]
    one = jnp.ones((t, QK_DIM - ROPE_DIM), F32)
    zero = jnp.zeros((t, QK_DIM - ROPE_DIM), F32)
    zh = jnp.zeros((t, half), F32)
    c = jnp.concatenate([cos, cos, one], axis=-1)
    s1 = jnp.concatenate([-sin, zh, zero], axis=-1)
    s2 = jnp.concatenate([zh, sin, zero], axis=-1)
    dup = lambda a: jnp.concatenate([a, a], axis=-1)
    return dup(c), dup(s1), dup(s2)


def _lam(lq1, lk1, lq2, lk2, lam_init):
    a = jnp.sum(lq1[...] * lk1[...], axis=-1, keepdims=True)
    b = jnp.sum(lq2[...] * lk2[...], axis=-1, keepdims=True)
    return jnp.exp(a) - jnp.exp(b) + lam_init


def _subln(o, g, lam_init):
    ms = jnp.mean(o * o, axis=-1, keepdims=True)
    return (o * lax.rsqrt(ms + EPS) * g) * (1.0 - lam_init)


def _attn_kernel(lq1, lk1, lq2, lk2, sg_ref, q_ref, k_ref, v_ref, o_ref,
                 qq_ref, m_ref, l_ref, acc_ref, *, tq, lam_init):
    qi = pl.program_id(2)
    lane = lax.broadcasted_iota(jnp.int32, (tq, HEAD_W), 1)
    q = q_ref[...].astype(F32)
    qq_ref[0:tq, :] = jnp.where(lane < QK_DIM, q, 0.0).astype(BF16)
    qq_ref[tq:2 * tq, :] = jnp.where(lane >= QK_DIM, q, 0.0).astype(BF16)
    m_ref[...] = jnp.full(m_ref.shape, NEG, F32)
    l_ref[...] = jnp.zeros(l_ref.shape, F32)
    acc_ref[...] = jnp.zeros(acc_ref.shape, F32)

    def step(j, masked):
        start = pl.multiple_of(j * tq, tq)
        kblk = k_ref[pl.ds(start, tq), :]
        vblk = v_ref[pl.ds(start, tq), :]
        s = lax.dot_general(qq_ref[...], kblk, (((1,), (1,)), ((), ())),
                            preferred_element_type=F32)
        if masked:
            r = lax.broadcasted_iota(jnp.int32, s.shape, 0)
            c = lax.broadcasted_iota(jnp.int32, s.shape, 1)
            r = jnp.where(r >= tq, r - tq, r)
            s = jnp.where(c <= r, s, NEG)
        m_prev = m_ref[...]
        m_new = jnp.maximum(m_prev, jnp.max(s, axis=-1, keepdims=True))
        alpha = jnp.exp(m_prev - m_new)
        p = jnp.exp(s - m_new)
        l_ref[...] = alpha * l_ref[...] + jnp.sum(p, axis=-1, keepdims=True)
        acc_ref[...] = alpha * acc_ref[...] + jnp.dot(p.astype(BF16), vblk,
                                                      preferred_element_type=F32)
        m_ref[...] = m_new

    def body(j, carry):
        step(j, False)
        return carry

    lax.fori_loop(0, qi, body, 0)
    step(qi, True)

    accn = acc_ref[...] / l_ref[...]
    lam = _lam(lq1, lk1, lq2, lk2, lam_init)
    o = accn[0:tq, :] - lam * accn[tq:2 * tq, :]
    o_ref[...] = _subln(o, sg_ref[...], lam_init).astype(BF16)


def _prompt_attention(q_bf, k_bf, v_bf, lams, subln_g, batch, seq, lam_init, tq=256):
    m, w = q_bf.shape
    n_heads = w // HEAD_W
    nq = seq // tq
    small = lambda b, h, i: (0, 0)
    kv = lambda b, h, i: (b, h)
    qo = lambda b, h, i: (b * nq + i, h)
    return pl.pallas_call(
        functools.partial(_attn_kernel, tq=tq, lam_init=lam_init),
        grid=(batch, n_heads, nq),
        in_specs=[pl.BlockSpec((1, QK_DIM), small)] * 4
                 + [pl.BlockSpec((1, HEAD_W), small),
                    pl.BlockSpec((tq, HEAD_W), qo),
                    pl.BlockSpec((seq, HEAD_W), kv),
                    pl.BlockSpec((seq, HEAD_W), kv)],
        out_specs=pl.BlockSpec((tq, HEAD_W), qo),
        out_shape=jax.ShapeDtypeStruct((m, w), BF16),
        scratch_shapes=[pltpu.VMEM((2 * tq, HEAD_W), BF16),
                        pltpu.VMEM((2 * tq, 1), F32),
                        pltpu.VMEM((2 * tq, 1), F32),
                        pltpu.VMEM((2 * tq, HEAD_W), F32)],
        compiler_params=_cparams(("parallel", "parallel", "arbitrary")),
        name="prompt_attn",
    )(*lams, subln_g, q_bf, k_bf, v_bf)


def _sample_attn_kernel(pt_ref, lq1, lk1, lq2, lk2, sg_ref, q_ref, bias_ref, nbias_ref,
                        kn_ref, vn_ref, *rest, n_pp, lam_init):
    k_refs = rest[:n_pp]
    v_refs = rest[n_pp:2 * n_pp]
    o_ref = rest[2 * n_pp]
    m_ref, l_ref, acc_ref = rest[2 * n_pp + 1:]
    p_idx = pl.program_id(1)
    n_p = pl.num_programs(1)
    q = q_ref[...]

    def update(kb, vb, bias):
        s = lax.dot_general(q, kb, (((1,), (1,)), ((), ())), preferred_element_type=F32) + bias
        m_prev = m_ref[...]
        m_new = jnp.maximum(m_prev, jnp.max(s, axis=-1, keepdims=True))
        alpha = jnp.exp(m_prev - m_new)
        p = jnp.exp(s - m_new)
        l_ref[...] = alpha * l_ref[...] + jnp.sum(p, axis=-1, keepdims=True)
        acc_ref[...] = alpha * acc_ref[...] + jnp.dot(p.astype(BF16), vb,
                                                      preferred_element_type=F32)
        m_ref[...] = m_new

    @pl.when(p_idx == 0)
    def _():
        m_ref[...] = jnp.full(m_ref.shape, NEG, F32)
        l_ref[...] = jnp.zeros(l_ref.shape, F32)
        acc_ref[...] = jnp.zeros(acc_ref.shape, F32)
        update(kn_ref[...], vn_ref[...], nbias_ref[...])

    for i in range(n_pp):
        update(k_refs[i][...].astype(BF16), v_refs[i][...].astype(BF16), bias_ref[...])

    @pl.when(p_idx == n_p - 1)
    def _():
        accn = acc_ref[...] / l_ref[...]
        half = accn.shape[0] // 2
        lam = _lam(lq1, lk1, lq2, lk2, lam_init)
        o = accn[0:half, :] - lam * accn[half:, :]
        o_ref[...] = _subln(o, sg_ref[...], lam_init).astype(BF16)


def _sample_attention(q_s, kn_bf, vn_bf, cache_k, cache_v, page_table, lams, subln_g,
                      dec_batch, dec_seq, lam_init, n_pp=4):
    n_pool, page, n_heads, _ = cache_k.shape
    n_pages = page_table.shape[1]
    rows_q = dec_seq * n_heads
    page_rows = page * n_heads
    q4 = q_s.reshape(dec_batch, dec_seq, n_heads, HEAD_W).transpose(0, 2, 1, 3)
    q4 = q4.reshape(dec_batch, rows_q, HEAD_W)
    lane = jnp.arange(HEAD_W)
    qt = jnp.concatenate([jnp.where(lane < QK_DIM, q4, 0), jnp.where(lane >= QK_DIM, q4, 0)],
                         axis=1).astype(BF16)
    pad = lambda a: jnp.pad(a.reshape(dec_batch, rows_q, HEAD_W),
                            ((0, 0), (0, LANES - rows_q), (0, 0)))
    kn, vn = pad(kn_bf), pad(vn_bf)
    c = jnp.arange(2 * rows_q)
    c_head = (c % rows_q) // dec_seq
    c_query = c % dec_seq
    r = jnp.arange(page_rows)
    bias = jnp.where((r % n_heads)[None, :] == c_head[:, None], 0.0, NEG).astype(F32)
    rn = jnp.arange(LANES)
    ok = ((rn % n_heads)[None, :] == c_head[:, None]) & ((rn // n_heads)[None, :] <= c_query[:, None]) \
        & (rn < rows_q)[None, :]
    nbias = jnp.where(ok, 0.0, NEG).astype(F32)
    ck = cache_k.reshape(n_pool, page_rows, HEAD_W)
    cv = cache_v.reshape(n_pool, page_rows, HEAD_W)

    small = lambda b, p, pt: (0, 0)
    per_b = lambda b, p, pt: (b, 0, 0)

    def page_spec(i):
        return pl.BlockSpec((None, page_rows, HEAD_W), lambda b, p, pt: (pt[b, p * n_pp + i], 0, 0))

    grid_spec = pltpu.PrefetchScalarGridSpec(
        num_scalar_prefetch=1,
        grid=(dec_batch, n_pages // n_pp),
        in_specs=[pl.BlockSpec((1, QK_DIM), small)] * 4
                 + [pl.BlockSpec((1, HEAD_W), small),
                    pl.BlockSpec((None, 2 * rows_q, HEAD_W), per_b),
                    pl.BlockSpec((2 * rows_q, page_rows), small),
                    pl.BlockSpec((2 * rows_q, LANES), small),
                    pl.BlockSpec((None, LANES, HEAD_W), per_b),
                    pl.BlockSpec((None, LANES, HEAD_W), per_b)]
                 + [page_spec(i) for i in range(n_pp)] * 2,
        out_specs=pl.BlockSpec((None, rows_q, HEAD_W), per_b),
        scratch_shapes=[pltpu.VMEM((2 * rows_q, 1), F32),
                        pltpu.VMEM((2 * rows_q, 1), F32),
                        pltpu.VMEM((2 * rows_q, HEAD_W), F32)],
    )
    o = pl.pallas_call(
        functools.partial(_sample_attn_kernel, n_pp=n_pp, lam_init=lam_init),
        grid_spec=grid_spec,
        out_shape=jax.ShapeDtypeStruct((dec_batch, rows_q, HEAD_W), BF16),
        compiler_params=_cparams(("parallel", "arbitrary")),
        name="sample_attn",
    )(page_table, *lams, subln_g, qt, bias, nbias, kn, vn, *([ck] * n_pp), *([cv] * n_pp))
    o = o.reshape(dec_batch, n_heads, dec_seq, HEAD_W).transpose(0, 2, 1, 3)
    return o.reshape(dec_batch * dec_seq, n_heads * HEAD_W)


def _ln_swish(y, g, b):
    mu = jnp.mean(y, axis=-1, keepdims=True)
    yc = y - mu
    var = jnp.mean(yc * yc, axis=-1, keepdims=True)
    z = yc * lax.rsqrt(var + EPS) * g + b
    return z * (1.0 / (1.0 + jnp.exp(-z)))


CONV_HALO = 32
CONV_CHUNK = 16


def _conv_prompt_kernel(cur_ref, tail_ref, w_ref, cb_ref, g_ref, b_ref, o_ref, ext_ref, *, tt):
    t = pl.program_id(1)
    tail = tail_ref[...]
    ext_ref[0:CONV_HALO, :] = jnp.where(t == 0, jnp.zeros_like(tail), tail)
    ext_ref[CONV_HALO:CONV_HALO + tt, :] = cur_ref[...]
    off = CONV_HALO - (CONV_K - 1)

    for i in range(tt // CONV_CHUNK):
        base = i * CONV_CHUNK
        acc = jnp.zeros((CONV_CHUNK, cur_ref.shape[1]), F32)
        for j in range(CONV_K):
            lo = base + off + j
            acc = acc + ext_ref[lo:lo + CONV_CHUNK, :] * w_ref[j:j + 1, :]
        y = acc + cb_ref[...]
        o_ref[base:base + CONV_CHUNK, :] = _ln_swish(y, g_ref[...], b_ref[...]).astype(BF16)


def _conv_prompt(u, conv_w, conv_b, ln_g, ln_b, batch, seq, tt=256):
    m, c = u.shape
    nt = seq // tt
    ratio = tt // CONV_HALO
    cur = lambda b, t: (b * nt + t, 0)
    tail = lambda b, t: (jnp.maximum((b * nt + t) * ratio - 1, 0), 0)
    small = lambda b, t: (0, 0)
    return pl.pallas_call(
        functools.partial(_conv_prompt_kernel, tt=tt),
        grid=(batch, nt),
        in_specs=[pl.BlockSpec((tt, c), cur),
                  pl.BlockSpec((CONV_HALO, c), tail),
                  pl.BlockSpec((CONV_K, c), small),
                  pl.BlockSpec((1, c), small),
                  pl.BlockSpec((1, c), small),
                  pl.BlockSpec((1, c), small)],
        out_specs=pl.BlockSpec((tt, c), cur),
        out_shape=jax.ShapeDtypeStruct((m, c), BF16),
        scratch_shapes=[pltpu.VMEM((CONV_HALO + tt, c), F32)],
        compiler_params=_cparams(("parallel", "parallel")),
        name="conv_prompt",
    )(u, u, conv_w, conv_b, ln_g, ln_b)


def _conv_sample_kernel(ext_ref, w_ref, cb_ref, g_ref, b_ref, o_ref):
    rows = o_ref.shape[0]
    acc = jnp.zeros(o_ref.shape, F32)
    for j in range(CONV_K):
        acc = acc + ext_ref[j:j + rows, :] * w_ref[j:j + 1, :]
    y = acc + cb_ref[...]
    o_ref[...] = _ln_swish(y, g_ref[...], b_ref[...]).astype(BF16)


def _conv_sample(ext, conv_w, conv_b, ln_g, ln_b, dec_seq):
    nb, n_rows, c = ext.shape
    out_rows = 8
    assert dec_seq <= out_rows
    pad_rows = (CONV_K - 1) + out_rows
    ext_p = jnp.pad(ext, ((0, 0), (0, pad_rows - n_rows), (0, 0)))
    small = lambda b: (0, 0)
    o = pl.pallas_call(
        _conv_sample_kernel,
        grid=(nb,),
        in_specs=[pl.BlockSpec((None, pad_rows, c), lambda b: (b, 0, 0)),
                  pl.BlockSpec((CONV_K, c), small),
                  pl.BlockSpec((1, c), small),
                  pl.BlockSpec((1, c), small),
                  pl.BlockSpec((1, c), small)],
        out_specs=pl.BlockSpec((None, out_rows, c), lambda b: (b, 0, 0)),
        out_shape=jax.ShapeDtypeStruct((nb, out_rows, c), BF16),
        compiler_params=_cparams(("parallel",)),
        name="conv_sample",
    )(ext_p, conv_w, conv_b, ln_g, ln_b)
    return o[:, :dec_seq].reshape(nb * dec_seq, c)


def _outproj_kernel(o_ref, c_ref, x_ref, wa_ref, wc_ref, g_ref, h_ref, hn_ref):
    h = x_ref[...] + jnp.dot(o_ref[...], wa_ref[...], preferred_element_type=F32) \
        + jnp.dot(c_ref[...], wc_ref[...], preferred_element_type=F32)
    h_ref[...] = h
    ms = jnp.mean(h * h, axis=-1, keepdims=True)
    hn_ref[...] = (h * lax.rsqrt(ms + EPS) * g_ref[...]).astype(BF16)


def _outproj(o_n, c, x, w_out_bf, g2, tm):
    m, d = x.shape
    wa = o_n.shape[1]
    row = lambda i: (i, 0)
    return pl.pallas_call(
        _outproj_kernel,
        grid=(m // tm,),
        in_specs=[pl.BlockSpec((tm, wa), row),
                  pl.BlockSpec((tm, c.shape[1]), row),
                  pl.BlockSpec((tm, d), row),
                  pl.BlockSpec((wa, d), lambda i: (0, 0)),
                  pl.BlockSpec((c.shape[1], d), lambda i: (wa // c.shape[1], 0)),
                  pl.BlockSpec((1, d), lambda i: (0, 0))],
        out_specs=[pl.BlockSpec((tm, d), row), pl.BlockSpec((tm, d), row)],
        out_shape=[jax.ShapeDtypeStruct((m, d), F32), jax.ShapeDtypeStruct((m, d), BF16)],
        compiler_params=_cparams(("parallel",)),
        name="outproj",
    )(o_n, c, x, w_out_bf, w_out_bf, g2)


def _ffn_kernel(hn_ref, h_ref, wu_ref, wd_ref, g_ref, y_ref, acc_ref):
    f = pl.program_id(1)

    @pl.when(f == 0)
    def _():
        acc_ref[...] = jnp.zeros(acc_ref.shape, F32)

    a = jnp.maximum(jnp.dot(hn_ref[...], wu_ref[...], preferred_element_type=F32), 0.0)
    acc_ref[...] += jnp.dot((a * a).astype(BF16), wd_ref[...], preferred_element_type=F32)

    @pl.when(f == pl.num_programs(1) - 1)
    def _():
        y = h_ref[...] + acc_ref[...]
        ms = jnp.mean(y * y, axis=-1, keepdims=True)
        y_ref[...] = y * lax.rsqrt(ms + EPS) * g_ref[...]


def _ffn(hn, h, w_up_bf, w_down_bf, final_g, tm, tf):
    m, d = h.shape
    dff = w_up_bf.shape[1]
    row = lambda i, f: (i, 0)
    return pl.pallas_call(
        _ffn_kernel,
        grid=(m // tm, dff // tf),
        in_specs=[pl.BlockSpec((tm, d), row),
                  pl.BlockSpec((tm, d), row),
                  pl.BlockSpec((d, tf), lambda i, f: (0, f)),
                  pl.BlockSpec((tf, d), lambda i, f: (f, 0)),
                  pl.BlockSpec((1, d), lambda i, f: (0, 0))],
        out_specs=pl.BlockSpec((tm, d), row),
        out_shape=jax.ShapeDtypeStruct((m, d), F32),
        scratch_shapes=[pltpu.VMEM((tm, d), F32)],
        compiler_params=_cparams(("parallel", "arbitrary")),
        name="ffn",
    )(hn, h, w_up_bf, w_down_bf, final_g)


def kernel(x_prompt, x_sample, cache_k, cache_v, state_conv, page_table, norm1_g, w_in,
           lambda_q1, lambda_k1, lambda_q2, lambda_k2, subln_g, conv_w, conv_b,
           conv_ln_g, conv_ln_b, w_out, norm2_g, w_up, w_down, final_g):
    depth = w_in.shape[0]
    assert depth == 1, "final norm is fused into the (single) layer's FFN kernel"
    batch, seq, d = x_prompt.shape
    dec_batch, dec_seq, _ = x_sample.shape
    n_heads, head_w = cache_k.shape[3], cache_k.shape[4]
    assert head_w == HEAD_W
    past = page_table.shape[1] * cache_k.shape[2]
    conv_c = conv_w.shape[2]

    xp = x_prompt.reshape(batch * seq, d)
    xs = x_sample.reshape(dec_batch * dec_seq, d)
    tm_p = 512
    tm_s = dec_batch * dec_seq

    tabs_p = _rope_tables(jnp.arange(seq, dtype=jnp.int32))
    pos_s = past + jnp.arange(dec_seq, dtype=jnp.int32)
    tabs_s = tuple(jnp.tile(t, (dec_batch, 1)) for t in _rope_tables(pos_s))

    l = 0
    lam_init = 0.8 - 0.6 * math.exp(-0.3 * l)
    row = lambda a: a[l].reshape(1, -1)
    lams = (row(lambda_q1), row(lambda_k1), row(lambda_q2), row(lambda_k2))
    g1, g2, sg = row(norm1_g), row(norm2_g), row(subln_g)
    cb, lg, lb = row(conv_b), row(conv_ln_g), row(conv_ln_b)
    fg = final_g.reshape(1, -1)
    w_in_bf = w_in[l].astype(BF16)
    w_out_bf = w_out[l].astype(BF16)
    w_up_bf = w_up[l].astype(BF16)
    w_down_bf = w_down[l].astype(BF16)
    cw = conv_w[l]

    q_p, k_p, kb_p, v_p, vb_p, u_p = _inproj(xp, g1, w_in_bf, tabs_p, tm_p, seq // tm_p)
    o_p = _prompt_attention(q_p, kb_p, vb_p, lams, sg, batch, seq, lam_init)
    c_p = _conv_prompt(u_p, cw, cb, lg, lb, batch, seq)
    h_p, hn_p = _outproj(o_p, c_p, xp, w_out_bf, g2, 256)
    y_p = _ffn(hn_p, h_p, w_up_bf, w_down_bf, fg, tm_p, 1024)

    q_s, k_s, kb_s, v_s, vb_s, u_s = _inproj(xs, g1, w_in_bf, tabs_s, tm_s, 1)
    o_s = _sample_attention(q_s, kb_s, vb_s, cache_k[l], cache_v[l], page_table, lams, sg,
                            dec_batch, dec_seq, lam_init)
    ext_s = jnp.concatenate([state_conv[l], u_s.reshape(dec_batch, dec_seq, conv_c)], axis=1)
    c_s = _conv_sample(ext_s, cw, cb, lg, lb, dec_seq)
    h_s, hn_s = _outproj(o_s, c_s, xs, w_out_bf, g2, tm_s)
    y_s = _ffn(hn_s, h_s, w_up_bf, w_down_bf, fg, tm_s, 1024)

    keep = CONV_K - 1
    return (y_p.reshape(batch, seq, d),
            y_s.reshape(dec_batch, dec_seq, d),
            k_p.reshape(1, batch, seq, n_heads, head_w),
            v_p.reshape(1, batch, seq, n_heads, head_w),
            u_p.reshape(batch, seq, conv_c)[:, seq - keep:][None],
            k_s.reshape(1, dec_batch, dec_seq, n_heads, head_w),
            v_s.reshape(1, dec_batch, dec_seq, n_heads, head_w),
            ext_s[:, dec_seq:][None])
```

```python
import functools
import math

import jax
import jax.numpy as jnp
from jax import lax
from jax.experimental import pallas as pl
from jax.experimental.pallas import tpu as pltpu

F32 = jnp.float32
BF16 = jnp.bfloat16

EPS = 1e-6
ROPE_THETA = 500000.0
LANES = 128
HEAD_W = 128
QK_DIM = 64
ROPE_DIM = 16
CONV_K = 31
NEG = -1e30
VMEM_LIMIT = 56 * 1024 * 1024


def _cparams(sem):
    return pltpu.CompilerParams(dimension_semantics=sem, vmem_limit_bytes=VMEM_LIMIT)


def _rope(z, c, s1, s2):
    return z * c + pltpu.roll(z, LANES - 8, 1) * s1 + pltpu.roll(z, 8, 1) * s2


def _inproj_kernel(x_ref, g_ref, w_ref, c_ref, s1_ref, s2_ref,
                   q_ref, k_ref, kb_ref, v_ref, vb_ref, u_ref, xn_ref, gv_ref):
    j = pl.program_id(1)

    @pl.when(j == 0)
    def _():
        x = x_ref[...]
        ms = jnp.mean(x * x, axis=-1, keepdims=True)
        xn_ref[...] = (x * lax.rsqrt(ms + EPS) * g_ref[...]).astype(BF16)

    z = jnp.dot(xn_ref[...], w_ref[...], preferred_element_type=F32)
    n_heads = z.shape[1] // HEAD_W

    @pl.when(j == 0)
    def _():
        c, s1, s2 = c_ref[...], s1_ref[...], s2_ref[...]
        for h in range(n_heads):
            sl = slice(h * HEAD_W, (h + 1) * HEAD_W)
            q_ref[:, sl] = (_rope(z[:, sl], c, s1, s2) * (QK_DIM ** -0.5)).astype(BF16)

    @pl.when(j == 1)
    def _():
        c, s1, s2 = c_ref[...], s1_ref[...], s2_ref[...]
        for h in range(n_heads):
            sl = slice(h * HEAD_W, (h + 1) * HEAD_W)
            kr = _rope(z[:, sl], c, s1, s2)
            k_ref[:, sl] = kr
            kb_ref[:, sl] = kr.astype(BF16)

    @pl.when(j == 2)
    def _():
        v_ref[...] = z
        vb_ref[...] = z.astype(BF16)

    @pl.when(j == 3)
    def _():
        gv_ref[...] = z

    @pl.when(j == 4)
    def _():
        u_ref[...] = gv_ref[...] * (1.0 / (1.0 + jnp.exp(-z)))


def _inproj(x, g, w_bf, tabs, tm, n_pos_blocks):
    m, d = x.shape
    nw = w_bf.shape[1] // 5
    row = lambda i, j: (i, 0)
    tab = lambda i, j: (i % n_pos_blocks, 0)
    out_f = jax.ShapeDtypeStruct((m, nw), F32)
    out_b = jax.ShapeDtypeStruct((m, nw), BF16)
    return pl.pallas_call(
        _inproj_kernel,
        grid=(m // tm, 5),
        in_specs=[pl.BlockSpec((tm, d), row),
                  pl.BlockSpec((1, d), lambda i, j: (0, 0)),
                  pl.BlockSpec((d, nw), lambda i, j: (0, j)),
                  pl.BlockSpec((tm, LANES), tab),
                  pl.BlockSpec((tm, LANES), tab),
                  pl.BlockSpec((tm, LANES), tab)],
        out_specs=[pl.BlockSpec((tm, nw), row)] * 6,
        out_shape=[out_b, out_f, out_b, out_f, out_b, out_f],
        scratch_shapes=[pltpu.VMEM((tm, d), BF16), pltpu.VMEM((tm, nw), F32)],
        compiler_params=_cparams(("parallel", "arbitrary")),
        name="inproj",
    )(x, g, w_bf, *tabs)


def _rope_tables(pos):
    half = ROPE_DIM // 2
    inv = jnp.power(ROPE_THETA, -jnp.arange(half, dtype=F32) * 2.0 / ROPE_DIM)
    ang = pos.astype(F32)[:, None] * inv[None, :]
    cos, sin = jnp.cos(ang), jnp.sin(ang)
    t = pos.shape[0]
    one = jnp.ones((t, QK_DIM - ROPE_DIM), F32)
    zero = jnp.zeros((t, QK_DIM - ROPE_DIM), F32)
    zh = jnp.zeros((t, half), F32)
    c = jnp.concatenate([cos, cos, one], axis=-1)
    s1 = jnp.concatenate([-sin, zh, zero], axis=-1)
    s2 = jnp.concatenate([zh, sin, zero], axis=-1)
    dup = lambda a: jnp.concatenate([a, a], axis=-1)
    return dup(c), dup(s1), dup(s2)


def _lam(lq1, lk1, lq2, lk2, lam_init):
    a = jnp.sum(lq1[...] * lk1[...], axis=-1, keepdims=True)
    b = jnp.sum(lq2[...] * lk2[...], axis=-1, keepdims=True)
    return jnp.exp(a) - jnp.exp(b) + lam_init


def _subln(o, g, lam_init):
    ms = jnp.mean(o * o, axis=-1, keepdims=True)
    return (o * lax.rsqrt(ms + EPS) * g) * (1.0 - lam_init)


def _fold_lanes(x, op):
    f = x[:, 0:LANES]
    for t in range(1, x.shape[1] // LANES):
        f = op(f, x[:, t * LANES:(t + 1) * LANES])
    return f


def _attn_kernel(lq1, lk1, lq2, lk2, sg_ref, q_ref, k_ref, v_ref, o_ref,
                 qq_ref, s_ref, *, tq, nq, lam_init):
    qi = pl.program_id(2)
    lane = lax.broadcasted_iota(jnp.int32, (tq, HEAD_W), 1)
    q = q_ref[...].astype(F32)
    qq_ref[0:tq, :] = jnp.where(lane < QK_DIM, q, 0.0).astype(BF16)
    qq_ref[tq:2 * tq, :] = jnp.where(lane >= QK_DIM, q, 0.0).astype(BF16)

    def variant(n_blk):
        mx = None
        for c in range(n_blk):
            cols = slice(c * tq, (c + 1) * tq)
            s = lax.dot_general(qq_ref[...], k_ref[cols, :], (((1,), (1,)), ((), ())),
                                preferred_element_type=F32)
            if c == n_blk - 1:
                r = lax.broadcasted_iota(jnp.int32, s.shape, 0)
                col = lax.broadcasted_iota(jnp.int32, s.shape, 1)
                r = jnp.where(r >= tq, r - tq, r)
                s = jnp.where(col <= r, s, NEG)
            s_ref[:, cols] = s
            f = _fold_lanes(s, jnp.maximum)
            mx = f if mx is None else jnp.maximum(mx, f)
        m = jnp.max(mx, axis=-1, keepdims=True)
        ls = None
        pv = None
        for c in range(n_blk):
            cols = slice(c * tq, (c + 1) * tq)
            p = jnp.exp(s_ref[:, cols] - m)
            f = _fold_lanes(p, jnp.add)
            ls = f if ls is None else ls + f
            d = jnp.dot(p.astype(BF16), v_ref[cols, :], preferred_element_type=F32)
            pv = d if pv is None else pv + d
        accn = pv / jnp.sum(ls, axis=-1, keepdims=True)
        lam = _lam(lq1, lk1, lq2, lk2, lam_init)
        o = accn[0:tq, :] - lam * accn[tq:2 * tq, :]
        o_ref[...] = _subln(o, sg_ref[...], lam_init).astype(BF16)

    for n in range(nq):
        pl.when(qi == n)(functools.partial(variant, n + 1))


def _prompt_attention(q_bf, k_bf, v_bf, lams, subln_g, batch, seq, lam_init, tq=256):
    m, w = q_bf.shape
    n_heads = w // HEAD_W
    nq = seq // tq
    small = lambda b, h, i: (0, 0)
    kv = lambda b, h, i: (b, h)
    qo = lambda b, h, i: (b * nq + i, h)
    return pl.pallas_call(
        functools.partial(_attn_kernel, tq=tq, nq=nq, lam_init=lam_init),
        grid=(batch, n_heads, nq),
        in_specs=[pl.BlockSpec((1, QK_DIM), small)] * 4
                 + [pl.BlockSpec((1, HEAD_W), small),
                    pl.BlockSpec((tq, HEAD_W), qo),
                    pl.BlockSpec((seq, HEAD_W), kv),
                    pl.BlockSpec((seq, HEAD_W), kv)],
        out_specs=pl.BlockSpec((tq, HEAD_W), qo),
        out_shape=jax.ShapeDtypeStruct((m, w), BF16),
        scratch_shapes=[pltpu.VMEM((2 * tq, HEAD_W), BF16),
                        pltpu.VMEM((2 * tq, seq), F32)],
        compiler_params=_cparams(("parallel", "parallel", "arbitrary")),
        name="prompt_attn",
    )(*lams, subln_g, q_bf, k_bf, v_bf)


def _sample_attn_kernel(pt_ref, lq1, lk1, lq2, lk2, sg_ref, q_ref, bias_ref, nbias_ref,
                        kn_ref, vn_ref, *rest, n_pp, lam_init):
    k_refs = rest[:n_pp]
    v_refs = rest[n_pp:2 * n_pp]
    o_ref = rest[2 * n_pp]
    m_ref, l_ref, acc_ref = rest[2 * n_pp + 1:]
    p_idx = pl.program_id(1)
    n_p = pl.num_programs(1)
    q = q_ref[...]

    def update(kbs, vbs, bias):
        ss = [lax.dot_general(q, kb, (((1,), (1,)), ((), ())), preferred_element_type=F32) + bias
              for kb in kbs]
        mx = _fold_lanes(ss[0], jnp.maximum)
        for s in ss[1:]:
            mx = jnp.maximum(mx, _fold_lanes(s, jnp.maximum))
        m_prev = m_ref[...]
        m_new = jnp.maximum(m_prev, jnp.max(mx, axis=-1, keepdims=True))
        alpha = jnp.exp(m_prev - m_new)
        ls = None
        pv = None
        for s, vb in zip(ss, vbs):
            p = jnp.exp(s - m_new)
            f = _fold_lanes(p, jnp.add)
            ls = f if ls is None else ls + f
            d = jnp.dot(p.astype(BF16), vb, preferred_element_type=F32)
            pv = d if pv is None else pv + d
        l_ref[...] = alpha * l_ref[...] + jnp.sum(ls, axis=-1, keepdims=True)
        acc_ref[...] = alpha * acc_ref[...] + pv
        m_ref[...] = m_new

    @pl.when(p_idx == 0)
    def _():
        m_ref[...] = jnp.full(m_ref.shape, NEG, F32)
        l_ref[...] = jnp.zeros(l_ref.shape, F32)
        acc_ref[...] = jnp.zeros(acc_ref.shape, F32)
        update([kn_ref[...]], [vn_ref[...]], nbias_ref[...])

    update([r[...].astype(BF16) for r in k_refs], [r[...].astype(BF16) for r in v_refs],
           bias_ref[...])

    @pl.when(p_idx == n_p - 1)
    def _():
        accn = acc_ref[...] / l_ref[...]
        half = accn.shape[0] // 2
        lam = _lam(lq1, lk1, lq2, lk2, lam_init)
        o = accn[0:half, :] - lam * accn[half:, :]
        o_ref[...] = _subln(o, sg_ref[...], lam_init).astype(BF16)


def _sample_attention(q_s, kn_bf, vn_bf, cache_k, cache_v, page_table, lams, subln_g,
                      dec_batch, dec_seq, lam_init, n_pp=8):
    n_pool, page, n_heads, _ = cache_k.shape
    n_pages = page_table.shape[1]
    rows_q = dec_seq * n_heads
    page_rows = page * n_heads
    q4 = q_s.reshape(dec_batch, dec_seq, n_heads, HEAD_W).transpose(0, 2, 1, 3)
    q4 = q4.reshape(dec_batch, rows_q, HEAD_W)
    lane = jnp.arange(HEAD_W)
    qt = jnp.concatenate([jnp.where(lane < QK_DIM, q4, 0), jnp.where(lane >= QK_DIM, q4, 0)],
                         axis=1).astype(BF16)
    pad = lambda a: jnp.pad(a.reshape(dec_batch, rows_q, HEAD_W),
                            ((0, 0), (0, LANES - rows_q), (0, 0)))
    kn, vn = pad(kn_bf), pad(vn_bf)
    c = jnp.arange(2 * rows_q)
    c_head = (c % rows_q) // dec_seq
    c_query = c % dec_seq
    r = jnp.arange(page_rows)
    bias = jnp.where((r % n_heads)[None, :] == c_head[:, None], 0.0, NEG).astype(F32)
    rn = jnp.arange(LANES)
    ok = ((rn % n_heads)[None, :] == c_head[:, None]) & ((rn // n_heads)[None, :] <= c_query[:, None]) \
        & (rn < rows_q)[None, :]
    nbias = jnp.where(ok, 0.0, NEG).astype(F32)
    ck = cache_k.reshape(n_pool, page_rows, HEAD_W)
    cv = cache_v.reshape(n_pool, page_rows, HEAD_W)

    small = lambda b, p, pt: (0, 0)
    per_b = lambda b, p, pt: (b, 0, 0)

    def page_spec(i):
        return pl.BlockSpec((None, page_rows, HEAD_W), lambda b, p, pt: (pt[b, p * n_pp + i], 0, 0))

    grid_spec = pltpu.PrefetchScalarGridSpec(
        num_scalar_prefetch=1,
        grid=(dec_batch, n_pages // n_pp),
        in_specs=[pl.BlockSpec((1, QK_DIM), small)] * 4
                 + [pl.BlockSpec((1, HEAD_W), small),
                    pl.BlockSpec((None, 2 * rows_q, HEAD_W), per_b),
                    pl.BlockSpec((2 * rows_q, page_rows), small),
                    pl.BlockSpec((2 * rows_q, LANES), small),
                    pl.BlockSpec((None, LANES, HEAD_W), per_b),
                    pl.BlockSpec((None, LANES, HEAD_W), per_b)]
                 + [page_spec(i) for i in range(n_pp)] * 2,
        out_specs=pl.BlockSpec((None, rows_q, HEAD_W), per_b),
        scratch_shapes=[pltpu.VMEM((2 * rows_q, 1), F32),
                        pltpu.VMEM((2 * rows_q, 1), F32),
                        pltpu.VMEM((2 * rows_q, HEAD_W), F32)],
    )
    o = pl.pallas_call(
        functools.partial(_sample_attn_kernel, n_pp=n_pp, lam_init=lam_init),
        grid_spec=grid_spec,
        out_shape=jax.ShapeDtypeStruct((dec_batch, rows_q, HEAD_W), BF16),
        compiler_params=_cparams(("parallel", "arbitrary")),
        name="sample_attn",
    )(page_table, *lams, subln_g, qt, bias, nbias, kn, vn, *([ck] * n_pp), *([cv] * n_pp))
    o = o.reshape(dec_batch, n_heads, dec_seq, HEAD_W).transpose(0, 2, 1, 3)
    return o.reshape(dec_batch * dec_seq, n_heads * HEAD_W)


def _ln_swish(y, g, b):
    mu = jnp.mean(y, axis=-1, keepdims=True)
    yc = y - mu
    var = jnp.mean(yc * yc, axis=-1, keepdims=True)
    z = yc * lax.rsqrt(var + EPS) * g + b
    return z * (1.0 / (1.0 + jnp.exp(-z)))


CONV_HALO = 32
CONV_CHUNK = 16


def _conv_prompt_kernel(cur_ref, tail_ref, w_ref, cb_ref, g_ref, b_ref, o_ref, ext_ref, y_ref, *, tt):
    t = pl.program_id(1)
    n_slab = cur_ref.shape[1] // LANES
    for sl in range(n_slab):
        lanes = slice(sl * LANES, (sl + 1) * LANES)
        tail = tail_ref[:, lanes]
        ext_ref[sl, 0:CONV_HALO, :] = jnp.where(t == 0, jnp.zeros_like(tail), tail)
        ext_ref[sl, CONV_HALO:CONV_HALO + tt, :] = cur_ref[:, lanes]
    off = CONV_HALO - (CONV_K - 1)
    n_ch = cur_ref.shape[1]

    def group(i, carry):
        base = i * CONV_CHUNK
        half = CONV_CHUNK // 2
        for sl in range(n_slab):
            lanes = slice(sl * LANES, (sl + 1) * LANES)
            xs = [ext_ref[sl, pl.ds(base + (off + k), half, stride=2), :] for k in range(CONV_K + 1)]
            even = jnp.zeros((half, LANES), F32)
            odd = jnp.zeros((half, LANES), F32)
            for j in range(CONV_K):
                w = w_ref[j:j + 1, lanes]
                even = even + xs[j] * w
                odd = odd + xs[j + 1] * w
            y_ref[sl, pl.ds(base, half, stride=2), :] = even + cb_ref[:, lanes]
            y_ref[sl, pl.ds(base + 1, half, stride=2), :] = odd + cb_ref[:, lanes]
        return carry

    lax.fori_loop(0, tt // CONV_CHUNK, group, 0)

    tot = y_ref[0]
    for sl in range(1, n_slab):
        tot = tot + y_ref[sl]
    mu = jnp.sum(tot, axis=-1, keepdims=True) * (1.0 / n_ch)
    sq = None
    for sl in range(n_slab):
        yc = y_ref[sl] - mu
        sq = yc * yc if sq is None else sq + yc * yc
    rstd = lax.rsqrt(jnp.sum(sq, axis=-1, keepdims=True) * (1.0 / n_ch) + EPS)
    for sl in range(n_slab):
        lanes = slice(sl * LANES, (sl + 1) * LANES)
        z = (y_ref[sl] - mu) * rstd * g_ref[:, lanes] + b_ref[:, lanes]
        o_ref[:, lanes] = (z * (1.0 / (1.0 + jnp.exp(-z)))).astype(BF16)


def _conv_prompt(u, conv_w, conv_b, ln_g, ln_b, batch, seq, tt=256):
    m, c = u.shape
    nt = seq // tt
    ratio = tt // CONV_HALO
    cur = lambda b, t: (b * nt + t, 0)
    tail = lambda b, t: (jnp.maximum((b * nt + t) * ratio - 1, 0), 0)
    small = lambda b, t: (0, 0)
    return pl.pallas_call(
        functools.partial(_conv_prompt_kernel, tt=tt),
        grid=(batch, nt),
        in_specs=[pl.BlockSpec((tt, c), cur),
                  pl.BlockSpec((CONV_HALO, c), tail),
                  pl.BlockSpec((CONV_K, c), small),
                  pl.BlockSpec((1, c), small),
                  pl.BlockSpec((1, c), small),
                  pl.BlockSpec((1, c), small)],
        out_specs=pl.BlockSpec((tt, c), cur),
        out_shape=jax.ShapeDtypeStruct((m, c), BF16),
        scratch_shapes=[pltpu.VMEM((c // LANES, CONV_HALO + tt, LANES), F32),
                        pltpu.VMEM((c // LANES, tt, LANES), F32)],
        compiler_params=_cparams(("parallel", "parallel")),
        name="conv_prompt",
    )(u, u, conv_w, conv_b, ln_g, ln_b)


def _conv_sample_kernel(ext_ref, w_ref, cb_ref, g_ref, b_ref, o_ref):
    rows = o_ref.shape[0]
    acc = jnp.zeros(o_ref.shape, F32)
    for j in range(CONV_K):
        acc = acc + ext_ref[j:j + rows, :] * w_ref[j:j + 1, :]
    y = acc + cb_ref[...]
    o_ref[...] = _ln_swish(y, g_ref[...], b_ref[...]).astype(BF16)


def _conv_sample(ext, conv_w, conv_b, ln_g, ln_b, dec_seq):
    nb, n_rows, c = ext.shape
    out_rows = 8
    assert dec_seq <= out_rows
    pad_rows = (CONV_K - 1) + out_rows
    ext_p = jnp.pad(ext, ((0, 0), (0, pad_rows - n_rows), (0, 0)))
    small = lambda b: (0, 0)
    o = pl.pallas_call(
        _conv_sample_kernel,
        grid=(nb,),
        in_specs=[pl.BlockSpec((None, pad_rows, c), lambda b: (b, 0, 0)),
                  pl.BlockSpec((CONV_K, c), small),
                  pl.BlockSpec((1, c), small),
                  pl.BlockSpec((1, c), small),
                  pl.BlockSpec((1, c), small)],
        out_specs=pl.BlockSpec((None, out_rows, c), lambda b: (b, 0, 0)),
        out_shape=jax.ShapeDtypeStruct((nb, out_rows, c), BF16),
        compiler_params=_cparams(("parallel",)),
        name="conv_sample",
    )(ext_p, conv_w, conv_b, ln_g, ln_b)
    return o[:, :dec_seq].reshape(nb * dec_seq, c)


def _outproj_kernel(o_ref, c_ref, x_ref, wa_ref, wc_ref, g_ref, h_ref, hn_ref):
    h = x_ref[...] + jnp.dot(o_ref[...], wa_ref[...], preferred_element_type=F32) \
        + jnp.dot(c_ref[...], wc_ref[...], preferred_element_type=F32)
    h_ref[...] = h
    ms = jnp.mean(h * h, axis=-1, keepdims=True)
    hn_ref[...] = (h * lax.rsqrt(ms + EPS) * g_ref[...]).astype(BF16)


def _outproj(o_n, c, x, w_out_bf, g2, tm):
    m, d = x.shape
    wa = o_n.shape[1]
    row = lambda i: (i, 0)
    return pl.pallas_call(
        _outproj_kernel,
        grid=(m // tm,),
        in_specs=[pl.BlockSpec((tm, wa), row),
                  pl.BlockSpec((tm, c.shape[1]), row),
                  pl.BlockSpec((tm, d), row),
                  pl.BlockSpec((wa, d), lambda i: (0, 0)),
                  pl.BlockSpec((c.shape[1], d), lambda i: (wa // c.shape[1], 0)),
                  pl.BlockSpec((1, d), lambda i: (0, 0))],
        out_specs=[pl.BlockSpec((tm, d), row), pl.BlockSpec((tm, d), row)],
        out_shape=[jax.ShapeDtypeStruct((m, d), F32), jax.ShapeDtypeStruct((m, d), BF16)],
        compiler_params=_cparams(("parallel",)),
        name="outproj",
    )(o_n, c, x, w_out_bf, w_out_bf, g2)


def _ffn_kernel(hn_ref, h_ref, wu_ref, wd_ref, g_ref, y_ref, acc_ref):
    f = pl.program_id(1)

    @pl.when(f == 0)
    def _():
        acc_ref[...] = jnp.zeros(acc_ref.shape, F32)

    a = jnp.maximum(jnp.dot(hn_ref[...], wu_ref[...], preferred_element_type=F32), 0.0)
    acc_ref[...] += jnp.dot((a * a).astype(BF16), wd_ref[...], preferred_element_type=F32)

    @pl.when(f == pl.num_programs(1) - 1)
    def _():
        y = h_ref[...] + acc_ref[...]
        ms = jnp.mean(y * y, axis=-1, keepdims=True)
        y_ref[...] = y * lax.rsqrt(ms + EPS) * g_ref[...]


def _ffn(hn, h, w_up_bf, w_down_bf, final_g, tm, tf):
    m, d = h.shape
    dff = w_up_bf.shape[1]
    row = lambda i, f: (i, 0)
    return pl.pallas_call(
        _ffn_kernel,
        grid=(m // tm, dff // tf),
        in_specs=[pl.BlockSpec((tm, d), row),
                  pl.BlockSpec((tm, d), row),
                  pl.BlockSpec((d, tf), lambda i, f: (0, f)),
                  pl.BlockSpec((tf, d), lambda i, f: (f, 0)),
                  pl.BlockSpec((1, d), lambda i, f: (0, 0))],
        out_specs=pl.BlockSpec((tm, d), row),
        out_shape=jax.ShapeDtypeStruct((m, d), F32),
        scratch_shapes=[pltpu.VMEM((tm, d), F32)],
        compiler_params=_cparams(("parallel", "arbitrary")),
        name="ffn",
    )(hn, h, w_up_bf, w_down_bf, final_g)


def kernel(x_prompt, x_sample, cache_k, cache_v, state_conv, page_table, norm1_g, w_in,
           lambda_q1, lambda_k1, lambda_q2, lambda_k2, subln_g, conv_w, conv_b,
           conv_ln_g, conv_ln_b, w_out, norm2_g, w_up, w_down, final_g):
    depth = w_in.shape[0]
    assert depth == 1, "final norm is fused into the (single) layer's FFN kernel"
    batch, seq, d = x_prompt.shape
    dec_batch, dec_seq, _ = x_sample.shape
    n_heads, head_w = cache_k.shape[3], cache_k.shape[4]
    assert head_w == HEAD_W
    past = page_table.shape[1] * cache_k.shape[2]
    conv_c = conv_w.shape[2]

    xp = x_prompt.reshape(batch * seq, d)
    xs = x_sample.reshape(dec_batch * dec_seq, d)
    tm_p = 512
    tm_s = dec_batch * dec_seq

    tabs_p = _rope_tables(jnp.arange(seq, dtype=jnp.int32))
    pos_s = past + jnp.arange(dec_seq, dtype=jnp.int32)
    tabs_s = tuple(jnp.tile(t, (dec_batch, 1)) for t in _rope_tables(pos_s))

    l = 0
    lam_init = 0.8 - 0.6 * math.exp(-0.3 * l)
    row = lambda a: a[l].reshape(1, -1)
    lams = (row(lambda_q1), row(lambda_k1), row(lambda_q2), row(lambda_k2))
    g1, g2, sg = row(norm1_g), row(norm2_g), row(subln_g)
    cb, lg, lb = row(conv_b), row(conv_ln_g), row(conv_ln_b)
    fg = final_g.reshape(1, -1)
    w_in_bf = w_in[l].astype(BF16)
    w_out_bf = w_out[l].astype(BF16)
    w_up_bf = w_up[l].astype(BF16)
    w_down_bf = w_down[l].astype(BF16)
    cw = conv_w[l]

    q_p, k_p, kb_p, v_p, vb_p, u_p = _inproj(xp, g1, w_in_bf, tabs_p, tm_p, seq // tm_p)
    o_p = _prompt_attention(q_p, kb_p, vb_p, lams, sg, batch, seq, lam_init)
    c_p = _conv_prompt(u_p, cw, cb, lg, lb, batch, seq)
    h_p, hn_p = _outproj(o_p, c_p, xp, w_out_bf, g2, 256)
    y_p = _ffn(hn_p, h_p, w_up_bf, w_down_bf, fg, tm_p, 1024)

    q_s, k_s, kb_s, v_s, vb_s, u_s = _inproj(xs, g1, w_in_bf, tabs_s, tm_s, 1)
    o_s = _sample_attention(q_s, kb_s, vb_s, cache_k[l], cache_v[l], page_table, lams, sg,
                            dec_batch, dec_seq, lam_init)
    ext_s = jnp.concatenate([state_conv[l], u_s.reshape(dec_batch, dec_seq, conv_c)], axis=1)
    c_s = _conv_sample(ext_s, cw, cb, lg, lb, dec_seq)
    h_s, hn_s = _outproj(o_s, c_s, xs, w_out_bf, g2, tm_s)
    y_s = _ffn(hn_s, h_s, w_up_bf, w_down_bf, fg, tm_s, 1024)

    keep = CONV_K - 1
    return (y_p.reshape(batch, seq, d),
            y_s.reshape(dec_batch, dec_seq, d),
            k_p.reshape(1, batch, seq, n_heads, head_w),
            v_p.reshape(1, batch, seq, n_heads, head_w),
            u_p.reshape(batch, seq, conv_c)[:, seq - keep:][None],
            k_s.reshape(1, dec_batch, dec_seq, n_heads, head_w),
            v_s.reshape(1, dec_batch, dec_seq, n_heads, head_w),
            ext_s[:, dec_seq:][None])
```

```python
import functools
import math

import numpy as np
import jax
import jax.numpy as jnp
from jax import lax
from jax.experimental import pallas as pl
from jax.experimental.pallas import tpu as pltpu

F32 = jnp.float32
BF16 = jnp.bfloat16

EPS = 1e-6
ROPE_THETA = 500000.0
LANES = 128
HEAD_W = 128
QK_DIM = 64
ROPE_DIM = 16
CONV_K = 31
NEG = -1e30
VMEM_LIMIT = 56 * 1024 * 1024


def _cparams(sem):
    return pltpu.CompilerParams(dimension_semantics=sem, vmem_limit_bytes=VMEM_LIMIT)


def _rope(z, c, s1, s2):
    return z * c + pltpu.roll(z, LANES - 8, 1) * s1 + pltpu.roll(z, 8, 1) * s2


def _inproj_kernel(x_ref, g_ref, w_ref, c_ref, s1_ref, s2_ref,
                   q_ref, k_ref, kb_ref, v_ref, vb_ref, u_ref, xn_ref, gv_ref):
    j = pl.program_id(1)

    @pl.when(j == 0)
    def _():
        x = x_ref[...]
        ms = jnp.mean(x * x, axis=-1, keepdims=True)
        xn_ref[...] = (x * lax.rsqrt(ms + EPS) * g_ref[...]).astype(BF16)

    z = jnp.dot(xn_ref[...], w_ref[...], preferred_element_type=F32)
    n_heads = z.shape[1] // HEAD_W
    c, s1, s2 = c_ref[...], s1_ref[...], s2_ref[...]
    zr = jnp.concatenate([_rope(z[:, h * HEAD_W:(h + 1) * HEAD_W], c, s1, s2)
                          for h in range(n_heads)], axis=1)

    @pl.when(j == 0)
    def _():
        q_ref[...] = (zr * (QK_DIM ** -0.5)).astype(BF16)

    @pl.when(j == 1)
    def _():
        k_ref[...] = zr
        kb_ref[...] = zr.astype(BF16)

    @pl.when(j == 2)
    def _():
        v_ref[...] = z
        vb_ref[...] = z.astype(BF16)

    @pl.when(j == 3)
    def _():
        gv_ref[...] = z

    @pl.when(j == 4)
    def _():
        u_ref[...] = gv_ref[...] * (1.0 / (1.0 + jnp.exp(-z)))


def _inproj(x, g, w_bf, tabs, tm, n_pos_blocks):
    m, d = x.shape
    nw = w_bf.shape[1] // 5
    row = lambda i, j: (i, 0)
    tab = lambda i, j: (i % n_pos_blocks, 0)
    out_f = jax.ShapeDtypeStruct((m, nw), F32)
    out_b = jax.ShapeDtypeStruct((m, nw), BF16)
    return pl.pallas_call(
        _inproj_kernel,
        grid=(m // tm, 5),
        in_specs=[pl.BlockSpec((tm, d), row),
                  pl.BlockSpec((1, d), lambda i, j: (0, 0)),
                  pl.BlockSpec((d, nw), lambda i, j: (0, j)),
                  pl.BlockSpec((tm, LANES), tab),
                  pl.BlockSpec((tm, LANES), tab),
                  pl.BlockSpec((tm, LANES), tab)],
        out_specs=[pl.BlockSpec((tm, nw), row)] * 6,
        out_shape=[out_b, out_f, out_b, out_f, out_b, out_f],
        scratch_shapes=[pltpu.VMEM((tm, d), BF16), pltpu.VMEM((tm, nw), F32)],
        compiler_params=_cparams(("parallel", "arbitrary")),
        name="inproj",
    )(x, g, w_bf, *tabs)


def _rope_tables(pos):
    pos = np.asarray(pos, np.float64)
    half = ROPE_DIM // 2
    inv = np.power(ROPE_THETA, -np.arange(half, dtype=np.float64) * 2.0 / ROPE_DIM)
    ang = pos[:, None] * inv[None, :]
    cos, sin = np.cos(ang), np.sin(ang)
    t = pos.shape[0]
    one = np.ones((t, QK_DIM - ROPE_DIM))
    zero = np.zeros((t, QK_DIM - ROPE_DIM))
    zh = np.zeros((t, half))
    c = np.concatenate([cos, cos, one], axis=-1)
    s1 = np.concatenate([-sin, zh, zero], axis=-1)
    s2 = np.concatenate([zh, sin, zero], axis=-1)
    dup = lambda a: jnp.asarray(np.concatenate([a, a], axis=-1), F32)
    return dup(c), dup(s1), dup(s2)


def _lam(lq1, lk1, lq2, lk2, lam_init):
    a = jnp.sum(lq1[...] * lk1[...], axis=-1, keepdims=True)
    b = jnp.sum(lq2[...] * lk2[...], axis=-1, keepdims=True)
    return jnp.exp(a) - jnp.exp(b) + lam_init


def _subln(o, g, lam_init):
    ms = jnp.mean(o * o, axis=-1, keepdims=True)
    return (o * lax.rsqrt(ms + EPS) * g) * (1.0 - lam_init)


def _fold_lanes(x, op):
    f = x[:, 0:LANES]
    for t in range(1, x.shape[1] // LANES):
        f = op(f, x[:, t * LANES:(t + 1) * LANES])
    return f


ATTN_SEG = 3


def _attn_kernel(lq1, lk1, lq2, lk2, sg_ref, qlo_ref, qhi_ref, k_ref, v_ref, o_ref,
                 qq_lo, qq_hi, s_lo, s_hi, *, tq, nq, lam_init):
    i = pl.program_id(2)
    lane = lax.broadcasted_iota(jnp.int32, (tq, HEAD_W), 1)
    for q_ref, qq_ref in ((qlo_ref, qq_lo), (qhi_ref, qq_hi)):
        q = q_ref[...].astype(F32)
        qq_ref[0:tq, :] = jnp.where(lane < QK_DIM, q, 0.0).astype(BF16)
        qq_ref[tq:2 * tq, :] = jnp.where(lane >= QK_DIM, q, 0.0).astype(BF16)

    def segment(qq_ref, s_ref, blocks, diag):
        mx = None
        for c in blocks:
            cols = slice(c * tq, (c + 1) * tq)
            s = lax.dot_general(qq_ref[...], k_ref[cols, :], (((1,), (1,)), ((), ())),
                                preferred_element_type=F32)
            if c == diag:
                r = lax.broadcasted_iota(jnp.int32, s.shape, 0)
                col = lax.broadcasted_iota(jnp.int32, s.shape, 1)
                r = jnp.where(r >= tq, r - tq, r)
                s = jnp.where(col <= r, s, NEG)
            s_ref[:, cols] = s
            f = _fold_lanes(s, jnp.maximum)
            mx = f if mx is None else jnp.maximum(mx, f)
        m = jnp.max(mx, axis=-1, keepdims=True)
        ls = None
        pv = None
        for c in blocks:
            cols = slice(c * tq, (c + 1) * tq)
            p = jnp.exp(s_ref[:, cols] - m)
            f = _fold_lanes(p, jnp.add)
            ls = f if ls is None else ls + f
            d = jnp.dot(p.astype(BF16), v_ref[cols, :], preferred_element_type=F32)
            pv = d if pv is None else pv + d
        return m, ls, pv

    def attend(qq_ref, s_ref, n_blk):
        parts = [segment(qq_ref, s_ref, range(c0, min(c0 + ATTN_SEG, n_blk)), n_blk - 1)
                 for c0 in range(0, n_blk, ATTN_SEG)]
        m = parts[0][0]
        for mp, _, _ in parts[1:]:
            m = jnp.maximum(m, mp)
        ls = None
        pv = None
        for mp, lsp, pvp in parts:
            w = jnp.exp(mp - m)
            ls = w * lsp if ls is None else ls + w * lsp
            pv = w * pvp if pv is None else pv + w * pvp
        accn = pv / jnp.sum(ls, axis=-1, keepdims=True)
        lam = _lam(lq1, lk1, lq2, lk2, lam_init)
        o = accn[0:tq, :] - lam * accn[tq:2 * tq, :]
        return _subln(o, sg_ref[...], lam_init).astype(BF16)

    def variant(n_lo, n_hi):
        o_ref[0:tq, :] = attend(qq_lo, s_lo, n_lo)
        o_ref[tq:2 * tq, :] = attend(qq_hi, s_hi, n_hi)

    for n in range(nq // 2):
        pl.when(i == n)(functools.partial(variant, n + 1, nq - n))


def _paired_block(r, nq):
    b, rb = r // nq, r % nq
    return b * nq + jnp.where(rb < nq // 2, 2 * rb, 2 * (nq - 1 - rb) + 1)


def _prompt_attention(q_bf, k_bf, v_bf, lams, subln_g, batch, seq, lam_init, tq):
    m, w = q_bf.shape
    n_heads = w // HEAD_W
    nq = seq // tq
    assert nq % 2 == 0
    half = nq // 2
    small = lambda b, h, i: (0, 0)
    kv = lambda b, h, i: (b, h)
    return pl.pallas_call(
        functools.partial(_attn_kernel, tq=tq, nq=nq, lam_init=lam_init),
        grid=(batch, n_heads, half),
        in_specs=[pl.BlockSpec((1, QK_DIM), small)] * 4
                 + [pl.BlockSpec((1, HEAD_W), small),
                    pl.BlockSpec((tq, HEAD_W), lambda b, h, i: (b * nq + i, h)),
                    pl.BlockSpec((tq, HEAD_W), lambda b, h, i: (b * nq + nq - 1 - i, h)),
                    pl.BlockSpec((seq, HEAD_W), kv),
                    pl.BlockSpec((seq, HEAD_W), kv)],
        out_specs=pl.BlockSpec((2 * tq, HEAD_W), lambda b, h, i: (b * half + i, h)),
        out_shape=jax.ShapeDtypeStruct((m, w), BF16),
        scratch_shapes=[pltpu.VMEM((2 * tq, HEAD_W), BF16),
                        pltpu.VMEM((2 * tq, HEAD_W), BF16),
                        pltpu.VMEM((2 * tq, half * tq), F32),
                        pltpu.VMEM((2 * tq, seq), F32)],
        compiler_params=_cparams(("parallel", "parallel", "arbitrary")),
        name="prompt_attn",
    )(*lams, subln_g, q_bf, q_bf, k_bf, v_bf)


def _sample_attn_kernel(pt_ref, lq1, lk1, lq2, lk2, sg_ref, q_ref, bias_ref, nbias_ref,
                        kn_ref, vn_ref, *rest, n_pp, lam_init):
    k_refs = rest[:n_pp]
    v_refs = rest[n_pp:2 * n_pp]
    o_ref = rest[2 * n_pp]
    m_ref, l_ref, acc_ref = rest[2 * n_pp + 1:]
    p_idx = pl.program_id(1)
    n_p = pl.num_programs(1)
    q = q_ref[...]

    def update(kbs, vbs, bias):
        ss = [lax.dot_general(q, kb, (((1,), (1,)), ((), ())), preferred_element_type=F32) + bias
              for kb in kbs]
        mx = _fold_lanes(ss[0], jnp.maximum)
        for s in ss[1:]:
            mx = jnp.maximum(mx, _fold_lanes(s, jnp.maximum))
        m_prev = m_ref[...]
        m_new = jnp.maximum(m_prev, jnp.max(mx, axis=-1, keepdims=True))
        alpha = jnp.exp(m_prev - m_new)
        ls = None
        pv = None
        for s, vb in zip(ss, vbs):
            p = jnp.exp(s - m_new)
            f = _fold_lanes(p, jnp.add)
            ls = f if ls is None else ls + f
            d = jnp.dot(p.astype(BF16), vb, preferred_element_type=F32)
            pv = d if pv is None else pv + d
        l_ref[...] = alpha * l_ref[...] + jnp.sum(ls, axis=-1, keepdims=True)
        acc_ref[...] = alpha * acc_ref[...] + pv
        m_ref[...] = m_new

    @pl.when(p_idx == 0)
    def _():
        m_ref[...] = jnp.full(m_ref.shape, NEG, F32)
        l_ref[...] = jnp.zeros(l_ref.shape, F32)
        acc_ref[...] = jnp.zeros(acc_ref.shape, F32)
        update([kn_ref[...]], [vn_ref[...]], nbias_ref[...])

    update([r[...].astype(BF16) for r in k_refs], [r[...].astype(BF16) for r in v_refs],
           bias_ref[...])

    @pl.when(p_idx == n_p - 1)
    def _():
        accn = acc_ref[...] / l_ref[...]
        half = accn.shape[0] // 2
        lam = _lam(lq1, lk1, lq2, lk2, lam_init)
        o = accn[0:half, :] - lam * accn[half:, :]
        o_ref[...] = _subln(o, sg_ref[...], lam_init).astype(BF16)


def _sample_attention(q_s, kn_bf, vn_bf, cache_k, cache_v, page_table, lams, subln_g,
                      dec_batch, dec_seq, lam_init, n_pp=16):
    n_pool, page, n_heads, _ = cache_k.shape
    n_pages = page_table.shape[1]
    rows_q = dec_seq * n_heads
    page_rows = page * n_heads
    q4 = q_s.reshape(dec_batch, dec_seq, n_heads, HEAD_W).transpose(0, 2, 1, 3)
    q4 = q4.reshape(dec_batch, rows_q, HEAD_W)
    lane = jnp.arange(HEAD_W)
    qt = jnp.concatenate([jnp.where(lane < QK_DIM, q4, 0), jnp.where(lane >= QK_DIM, q4, 0)],
                         axis=1).astype(BF16)
    pad = lambda a: jnp.pad(a.reshape(dec_batch, rows_q, HEAD_W),
                            ((0, 0), (0, LANES - rows_q), (0, 0)))
    kn, vn = pad(kn_bf), pad(vn_bf)
    c = np.arange(2 * rows_q)
    c_head = (c % rows_q) // dec_seq
    c_query = c % dec_seq
    r = np.arange(page_rows)
    bias = jnp.asarray(np.where((r % n_heads)[None, :] == c_head[:, None], 0.0, NEG), F32)
    rn = np.arange(LANES)
    ok = ((rn % n_heads)[None, :] == c_head[:, None]) & ((rn // n_heads)[None, :] <= c_query[:, None]) \
        & (rn < rows_q)[None, :]
    nbias = jnp.asarray(np.where(ok, 0.0, NEG), F32)
    ck = cache_k.reshape(n_pool, page_rows, HEAD_W)
    cv = cache_v.reshape(n_pool, page_rows, HEAD_W)

    small = lambda b, p, pt: (0, 0)
    per_b = lambda b, p, pt: (b, 0, 0)

    def page_spec(i):
        return pl.BlockSpec((None, page_rows, HEAD_W), lambda b, p, pt: (pt[b, p * n_pp + i], 0, 0))

    grid_spec = pltpu.PrefetchScalarGridSpec(
        num_scalar_prefetch=1,
        grid=(dec_batch, n_pages // n_pp),
        in_specs=[pl.BlockSpec((1, QK_DIM), small)] * 4
                 + [pl.BlockSpec((1, HEAD_W), small),
                    pl.BlockSpec((None, 2 * rows_q, HEAD_W), per_b),
                    pl.BlockSpec((2 * rows_q, page_rows), small),
                    pl.BlockSpec((2 * rows_q, LANES), small),
                    pl.BlockSpec((None, LANES, HEAD_W), per_b),
                    pl.BlockSpec((None, LANES, HEAD_W), per_b)]
                 + [page_spec(i) for i in range(n_pp)] * 2,
        out_specs=pl.BlockSpec((None, rows_q, HEAD_W), per_b),
        scratch_shapes=[pltpu.VMEM((2 * rows_q, 1), F32),
                        pltpu.VMEM((2 * rows_q, 1), F32),
                        pltpu.VMEM((2 * rows_q, HEAD_W), F32)],
    )
    o = pl.pallas_call(
        functools.partial(_sample_attn_kernel, n_pp=n_pp, lam_init=lam_init),
        grid_spec=grid_spec,
        out_shape=jax.ShapeDtypeStruct((dec_batch, rows_q, HEAD_W), BF16),
        compiler_params=_cparams(("parallel", "arbitrary")),
        name="sample_attn",
    )(page_table, *lams, subln_g, qt, bias, nbias, kn, vn, *([ck] * n_pp), *([cv] * n_pp))
    o = o.reshape(dec_batch, n_heads, dec_seq, HEAD_W).transpose(0, 2, 1, 3)
    return o.reshape(dec_batch * dec_seq, n_heads * HEAD_W)


def _ln_swish(y, g, b):
    mu = jnp.mean(y, axis=-1, keepdims=True)
    yc = y - mu
    var = jnp.mean(yc * yc, axis=-1, keepdims=True)
    z = yc * lax.rsqrt(var + EPS) * g + b
    return z * (1.0 / (1.0 + jnp.exp(-z)))


CONV_HALO = 32
CONV_CHUNK = 16


def _conv_prompt_kernel(cur_ref, tail_ref, w_ref, cb_ref, g_ref, b_ref, o_ref, ext_ref, y_ref, *, tt):
    t = pl.program_id(1)
    n_slab = cur_ref.shape[1] // LANES
    for sl in range(n_slab):
        lanes = slice(sl * LANES, (sl + 1) * LANES)
        tail = tail_ref[:, lanes]
        ext_ref[sl, 0:CONV_HALO, :] = jnp.where(t == 0, jnp.zeros_like(tail), tail)
        ext_ref[sl, CONV_HALO:CONV_HALO + tt, :] = cur_ref[:, lanes]
    off = CONV_HALO - (CONV_K - 1)
    n_ch = cur_ref.shape[1]

    def group(i, carry):
        base = i * CONV_CHUNK
        half = CONV_CHUNK // 2
        for sl in range(n_slab):
            lanes = slice(sl * LANES, (sl + 1) * LANES)
            xs = [ext_ref[sl, pl.ds(base + (off + k), half, stride=2), :] for k in range(CONV_K + 1)]
            even = jnp.zeros((half, LANES), F32)
            odd = jnp.zeros((half, LANES), F32)
            for j in range(CONV_K):
                w = w_ref[j:j + 1, lanes]
                even = even + xs[j] * w
                odd = odd + xs[j + 1] * w
            y_ref[sl, pl.ds(base, half, stride=2), :] = even + cb_ref[:, lanes]
            y_ref[sl, pl.ds(base + 1, half, stride=2), :] = odd + cb_ref[:, lanes]
        return carry

    lax.fori_loop(0, tt // CONV_CHUNK, group, 0)

    tot = y_ref[0]
    for sl in range(1, n_slab):
        tot = tot + y_ref[sl]
    mu = jnp.sum(tot, axis=-1, keepdims=True) * (1.0 / n_ch)
    sq = None
    for sl in range(n_slab):
        yc = y_ref[sl] - mu
        sq = yc * yc if sq is None else sq + yc * yc
    rstd = lax.rsqrt(jnp.sum(sq, axis=-1, keepdims=True) * (1.0 / n_ch) + EPS)
    for sl in range(n_slab):
        lanes = slice(sl * LANES, (sl + 1) * LANES)
        z = (y_ref[sl] - mu) * rstd * g_ref[:, lanes] + b_ref[:, lanes]
        o_ref[:, lanes] = (z * (1.0 / (1.0 + jnp.exp(-z)))).astype(BF16)


def _conv_prompt(u, conv_w, conv_b, ln_g, ln_b, batch, seq, tt=256):
    m, c = u.shape
    nt = seq // tt
    ratio = tt // CONV_HALO
    cur = lambda b, t: (b * nt + t, 0)
    tail = lambda b, t: (jnp.maximum((b * nt + t) * ratio - 1, 0), 0)
    small = lambda b, t: (0, 0)
    return pl.pallas_call(
        functools.partial(_conv_prompt_kernel, tt=tt),
        grid=(batch, nt),
        in_specs=[pl.BlockSpec((tt, c), cur),
                  pl.BlockSpec((CONV_HALO, c), tail),
                  pl.BlockSpec((CONV_K, c), small),
                  pl.BlockSpec((1, c), small),
                  pl.BlockSpec((1, c), small),
                  pl.BlockSpec((1, c), small)],
        out_specs=pl.BlockSpec((tt, c), cur),
        out_shape=jax.ShapeDtypeStruct((m, c), BF16),
        scratch_shapes=[pltpu.VMEM((c // LANES, CONV_HALO + tt, LANES), F32),
                        pltpu.VMEM((c // LANES, tt, LANES), F32)],
        compiler_params=_cparams(("parallel", "parallel")),
        name="conv_prompt",
    )(u, u, conv_w, conv_b, ln_g, ln_b)


def _conv_sample_kernel(ext_ref, w_ref, cb_ref, g_ref, b_ref, o_ref):
    nb, rows, _ = o_ref.shape
    for e in range(nb):
        acc = jnp.zeros(o_ref.shape[1:], F32)
        for j in range(CONV_K):
            acc = acc + ext_ref[e, j:j + rows, :] * w_ref[j:j + 1, :]
        y = acc + cb_ref[...]
        o_ref[e] = _ln_swish(y, g_ref[...], b_ref[...]).astype(BF16)


CONV_SAMPLE_ROWS = 8
CONV_SAMPLE_ENTRIES = 8


def _conv_sample(ext, conv_w, conv_b, ln_g, ln_b, dec_seq):
    nb, n_rows, c = ext.shape
    assert dec_seq <= CONV_SAMPLE_ROWS and nb % CONV_SAMPLE_ENTRIES == 0
    pad_rows = (CONV_K - 1) + CONV_SAMPLE_ROWS
    ext_p = jnp.pad(ext, ((0, 0), (0, pad_rows - n_rows), (0, 0)))
    small = lambda b: (0, 0)
    o = pl.pallas_call(
        _conv_sample_kernel,
        grid=(nb // CONV_SAMPLE_ENTRIES,),
        in_specs=[pl.BlockSpec((CONV_SAMPLE_ENTRIES, pad_rows, c), lambda b: (b, 0, 0)),
                  pl.BlockSpec((CONV_K, c), small),
                  pl.BlockSpec((1, c), small),
                  pl.BlockSpec((1, c), small),
                  pl.BlockSpec((1, c), small)],
        out_specs=pl.BlockSpec((CONV_SAMPLE_ENTRIES, CONV_SAMPLE_ROWS, c), lambda b: (b, 0, 0)),
        out_shape=jax.ShapeDtypeStruct((nb, CONV_SAMPLE_ROWS, c), BF16),
        compiler_params=_cparams(("parallel",)),
        name="conv_sample",
    )(ext_p, conv_w, conv_b, ln_g, ln_b)
    return o[:, :dec_seq].reshape(nb * dec_seq, c)


def _outproj_kernel(o_ref, c_ref, x_ref, wa_ref, wc_ref, g_ref, h_ref, hn_ref):
    h = x_ref[...] + jnp.dot(o_ref[...], wa_ref[...], preferred_element_type=F32) \
        + jnp.dot(c_ref[...], wc_ref[...], preferred_element_type=F32)
    h_ref[...] = h
    ms = jnp.mean(h * h, axis=-1, keepdims=True)
    hn_ref[...] = (h * lax.rsqrt(ms + EPS) * g_ref[...]).astype(BF16)


def _outproj(o_n, c, x, w_out_bf, g2, tm, o_block=None):
    m, d = x.shape
    wa = o_n.shape[1]
    row = lambda i: (i, 0)
    o_row = row if o_block is None else (lambda i: (o_block(i), 0))
    return pl.pallas_call(
        _outproj_kernel,
        grid=(m // tm,),
        in_specs=[pl.BlockSpec((tm, wa), o_row),
                  pl.BlockSpec((tm, c.shape[1]), row),
                  pl.BlockSpec((tm, d), row),
                  pl.BlockSpec((wa, d), lambda i: (0, 0)),
                  pl.BlockSpec((c.shape[1], d), lambda i: (wa // c.shape[1], 0)),
                  pl.BlockSpec((1, d), lambda i: (0, 0))],
        out_specs=[pl.BlockSpec((tm, d), row), pl.BlockSpec((tm, d), row)],
        out_shape=[jax.ShapeDtypeStruct((m, d), F32), jax.ShapeDtypeStruct((m, d), BF16)],
        compiler_params=_cparams(("parallel",)),
        name="outproj",
    )(o_n, c, x, w_out_bf, w_out_bf, g2)


def _ffn_kernel(hn_ref, h_ref, wu_ref, wd_ref, g_ref, y_ref, acc_ref):
    f = pl.program_id(1)

    @pl.when(f == 0)
    def _():
        acc_ref[...] = jnp.zeros(acc_ref.shape, F32)

    a = jnp.maximum(jnp.dot(hn_ref[...], wu_ref[...], preferred_element_type=F32), 0.0)
    acc_ref[...] += jnp.dot((a * a).astype(BF16), wd_ref[...], preferred_element_type=F32)

    @pl.when(f == pl.num_programs(1) - 1)
    def _():
        y = h_ref[...] + acc_ref[...]
        ms = jnp.mean(y * y, axis=-1, keepdims=True)
        y_ref[...] = y * lax.rsqrt(ms + EPS) * g_ref[...]


def _ffn(hn, h, w_up_bf, w_down_bf, final_g, tm, tf):
    m, d = h.shape
    dff = w_up_bf.shape[1]
    row = lambda i, f: (i, 0)
    return pl.pallas_call(
        _ffn_kernel,
        grid=(m // tm, dff // tf),
        in_specs=[pl.BlockSpec((tm, d), row),
                  pl.BlockSpec((tm, d), row),
                  pl.BlockSpec((d, tf), lambda i, f: (0, f)),
                  pl.BlockSpec((tf, d), lambda i, f: (f, 0)),
                  pl.BlockSpec((1, d), lambda i, f: (0, 0))],
        out_specs=pl.BlockSpec((tm, d), row),
        out_shape=jax.ShapeDtypeStruct((m, d), F32),
        scratch_shapes=[pltpu.VMEM((tm, d), F32)],
        compiler_params=_cparams(("parallel", "arbitrary")),
        name="ffn",
    )(hn, h, w_up_bf, w_down_bf, final_g)


def kernel(x_prompt, x_sample, cache_k, cache_v, state_conv, page_table, norm1_g, w_in,
           lambda_q1, lambda_k1, lambda_q2, lambda_k2, subln_g, conv_w, conv_b,
           conv_ln_g, conv_ln_b, w_out, norm2_g, w_up, w_down, final_g):
    depth = w_in.shape[0]
    assert depth == 1, "final norm is fused into the (single) layer's FFN kernel"
    batch, seq, d = x_prompt.shape
    dec_batch, dec_seq, _ = x_sample.shape
    n_heads, head_w = cache_k.shape[3], cache_k.shape[4]
    assert head_w == HEAD_W
    past = page_table.shape[1] * cache_k.shape[2]
    conv_c = conv_w.shape[2]

    xp = x_prompt.reshape(batch * seq, d)
    xs = x_sample.reshape(dec_batch * dec_seq, d)
    tm_p = 512
    tm_s = dec_batch * dec_seq

    tabs_p = _rope_tables(np.arange(seq))
    tabs_s = _rope_tables(np.tile(past + np.arange(dec_seq), dec_batch))

    l = 0
    lam_init = 0.8 - 0.6 * math.exp(-0.3 * l)
    row = lambda a: a[l].reshape(1, -1)
    lams = (row(lambda_q1), row(lambda_k1), row(lambda_q2), row(lambda_k2))
    g1, g2, sg = row(norm1_g), row(norm2_g), row(subln_g)
    cb, lg, lb = row(conv_b), row(conv_ln_g), row(conv_ln_b)
    fg = final_g.reshape(1, -1)
    w_in_bf = w_in[l].astype(BF16)
    w_out_bf = w_out[l].astype(BF16)
    w_up_bf = w_up[l].astype(BF16)
    w_down_bf = w_down[l].astype(BF16)
    cw = conv_w[l]

    q_p, k_p, kb_p, v_p, vb_p, u_p = _inproj(xp, g1, w_in_bf, tabs_p, tm_p, seq // tm_p)
    tq = 256
    o_p = _prompt_attention(q_p, kb_p, vb_p, lams, sg, batch, seq, lam_init, tq)
    c_p = _conv_prompt(u_p, cw, cb, lg, lb, batch, seq)
    h_p, hn_p = _outproj(o_p, c_p, xp, w_out_bf, g2, tq,
                         o_block=functools.partial(_paired_block, nq=seq // tq))
    y_p = _ffn(hn_p, h_p, w_up_bf, w_down_bf, fg, tm_p, 1024)

    q_s, k_s, kb_s, v_s, vb_s, u_s = _inproj(xs, g1, w_in_bf, tabs_s, tm_s, 1)
    o_s = _sample_attention(q_s, kb_s, vb_s, cache_k[l], cache_v[l], page_table, lams, sg,
                            dec_batch, dec_seq, lam_init)
    ext_s = jnp.concatenate([state_conv[l], u_s.reshape(dec_batch, dec_seq, conv_c)], axis=1)
    c_s = _conv_sample(ext_s, cw, cb, lg, lb, dec_seq)
    h_s, hn_s = _outproj(o_s, c_s, xs, w_out_bf, g2, tm_s)
    y_s = _ffn(hn_s, h_s, w_up_bf, w_down_bf, fg, tm_s, 1024)

    keep = CONV_K - 1
    return (y_p.reshape(batch, seq, d),
            y_s.reshape(dec_batch, dec_seq, d),
            k_p.reshape(1, batch, seq, n_heads, head_w),
            v_p.reshape(1, batch, seq, n_heads, head_w),
            u_p.reshape(batch, seq, conv_c)[:, seq - keep:][None],
            k_s.reshape(1, dec_batch, dec_seq, n_heads, head_w),
            v_s.reshape(1, dec_batch, dec_seq, n_heads, head_w),
            ext_s[:, dec_seq:][None])
```

```python
import functools
import math

import numpy as np
import jax
import jax.numpy as jnp
from jax import lax
from jax.experimental import pallas as pl
from jax.experimental.pallas import tpu as pltpu

F32 = jnp.float32
BF16 = jnp.bfloat16

EPS = 1e-6
ROPE_THETA = 500000.0
LANES = 128
HEAD_W = 128
QK_DIM = 64
ROPE_DIM = 16
CONV_K = 31
NEG = -1e30
VMEM_LIMIT = 56 * 1024 * 1024


def _cparams(sem):
    return pltpu.CompilerParams(dimension_semantics=sem, vmem_limit_bytes=VMEM_LIMIT)


def _rope(z, c, s1, s2):
    return z * c + pltpu.roll(z, LANES - 8, 1) * s1 + pltpu.roll(z, 8, 1) * s2


def _inproj_kernel(x_ref, g_ref, w_ref, c_ref, s1_ref, s2_ref,
                   q_ref, k_ref, kb_ref, v_ref, vb_ref, u_ref, xn_ref, gv_ref):
    j = pl.program_id(1)

    @pl.when(j == 0)
    def _():
        x = x_ref[...]
        ms = jnp.mean(x * x, axis=-1, keepdims=True)
        xn_ref[...] = (x * lax.rsqrt(ms + EPS) * g_ref[...]).astype(BF16)

    z = jnp.dot(xn_ref[...], w_ref[...], preferred_element_type=F32)
    n_heads = z.shape[1] // HEAD_W
    c, s1, s2 = c_ref[...], s1_ref[...], s2_ref[...]
    zr = jnp.concatenate([_rope(z[:, h * HEAD_W:(h + 1) * HEAD_W], c, s1, s2)
                          for h in range(n_heads)], axis=1)

    @pl.when(j == 0)
    def _():
        q_ref[...] = (zr * (QK_DIM ** -0.5)).astype(BF16)

    @pl.when(j == 1)
    def _():
        k_ref[...] = zr
        kb_ref[...] = zr.astype(BF16)

    @pl.when(j == 2)
    def _():
        v_ref[...] = z
        vb_ref[...] = z.astype(BF16)

    @pl.when(j == 3)
    def _():
        gv_ref[...] = z

    @pl.when(j == 4)
    def _():
        u_ref[...] = gv_ref[...] * (1.0 / (1.0 + jnp.exp(-z)))


def _inproj(x, g, w_bf, tabs, tm, n_pos_blocks):
    m, d = x.shape
    nw = w_bf.shape[1] // 5
    row = lambda i, j: (i, 0)
    tab = lambda i, j: (i % n_pos_blocks, 0)
    out_f = jax.ShapeDtypeStruct((m, nw), F32)
    out_b = jax.ShapeDtypeStruct((m, nw), BF16)
    return pl.pallas_call(
        _inproj_kernel,
        grid=(m // tm, 5),
        in_specs=[pl.BlockSpec((tm, d), row),
                  pl.BlockSpec((1, d), lambda i, j: (0, 0)),
                  pl.BlockSpec((d, nw), lambda i, j: (0, j)),
                  pl.BlockSpec((tm, LANES), tab),
                  pl.BlockSpec((tm, LANES), tab),
                  pl.BlockSpec((tm, LANES), tab)],
        out_specs=[pl.BlockSpec((tm, nw), row)] * 6,
        out_shape=[out_b, out_f, out_b, out_f, out_b, out_f],
        scratch_shapes=[pltpu.VMEM((tm, d), BF16), pltpu.VMEM((tm, nw), F32)],
        compiler_params=_cparams(("parallel", "arbitrary")),
        name="inproj",
    )(x, g, w_bf, *tabs)


def _rope_tables(pos):
    pos = np.asarray(pos, np.float64)
    half = ROPE_DIM // 2
    inv = np.power(ROPE_THETA, -np.arange(half, dtype=np.float64) * 2.0 / ROPE_DIM)
    ang = pos[:, None] * inv[None, :]
    cos, sin = np.cos(ang), np.sin(ang)
    t = pos.shape[0]
    one = np.ones((t, QK_DIM - ROPE_DIM))
    zero = np.zeros((t, QK_DIM - ROPE_DIM))
    zh = np.zeros((t, half))
    c = np.concatenate([cos, cos, one], axis=-1)
    s1 = np.concatenate([-sin, zh, zero], axis=-1)
    s2 = np.concatenate([zh, sin, zero], axis=-1)
    dup = lambda a: jnp.asarray(np.concatenate([a, a], axis=-1), F32)
    return dup(c), dup(s1), dup(s2)


def _lam(lq1, lk1, lq2, lk2, lam_init):
    a = jnp.sum(lq1[...] * lk1[...], axis=-1, keepdims=True)
    b = jnp.sum(lq2[...] * lk2[...], axis=-1, keepdims=True)
    return jnp.exp(a) - jnp.exp(b) + lam_init


def _subln(o, g, lam_init):
    ms = jnp.mean(o * o, axis=-1, keepdims=True)
    return (o * lax.rsqrt(ms + EPS) * g) * (1.0 - lam_init)


def _fold_lanes(x, op):
    f = x[:, 0:LANES]
    for t in range(1, x.shape[1] // LANES):
        f = op(f, x[:, t * LANES:(t + 1) * LANES])
    return f


ATTN_SEG = 3


def _attn_kernel(pt_ref, lq1, lk1, lq2, lk2, sg_ref, qlo_ref, qhi_ref, k_ref, v_ref,
                 qs_ref, bias_ref, nbias_ref, kn_ref, vn_ref, *rest, tq, nq, n_pp, lam_init):
    del pt_ref
    pk_refs = rest[:n_pp]
    pv_refs = rest[n_pp:2 * n_pp]
    o_ref, os_ref = rest[2 * n_pp:2 * n_pp + 2]
    qq_lo, qq_hi, s_lo, s_hi, m_ref, l_ref, acc_ref = rest[2 * n_pp + 2:]
    i = pl.program_id(2)
    lane = lax.broadcasted_iota(jnp.int32, (tq, HEAD_W), 1)
    for q_ref, qq_ref in ((qlo_ref, qq_lo), (qhi_ref, qq_hi)):
        q = q_ref[...].astype(F32)
        qq_ref[0:tq, :] = jnp.where(lane < QK_DIM, q, 0.0).astype(BF16)
        qq_ref[tq:2 * tq, :] = jnp.where(lane >= QK_DIM, q, 0.0).astype(BF16)

    def segment(qq_ref, s_ref, blocks, diag):
        mx = None
        for c in blocks:
            cols = slice(c * tq, (c + 1) * tq)
            s = lax.dot_general(qq_ref[...], k_ref[cols, :], (((1,), (1,)), ((), ())),
                                preferred_element_type=F32)
            if c == diag:
                r = lax.broadcasted_iota(jnp.int32, s.shape, 0)
                col = lax.broadcasted_iota(jnp.int32, s.shape, 1)
                r = jnp.where(r >= tq, r - tq, r)
                s = jnp.where(col <= r, s, NEG)
            s_ref[:, cols] = s
            f = _fold_lanes(s, jnp.maximum)
            mx = f if mx is None else jnp.maximum(mx, f)
        m = jnp.max(mx, axis=-1, keepdims=True)
        ls = None
        pv = None
        for c in blocks:
            cols = slice(c * tq, (c + 1) * tq)
            p = jnp.exp(s_ref[:, cols] - m)
            f = _fold_lanes(p, jnp.add)
            ls = f if ls is None else ls + f
            d = jnp.dot(p.astype(BF16), v_ref[cols, :], preferred_element_type=F32)
            pv = d if pv is None else pv + d
        return m, ls, pv

    def attend(qq_ref, s_ref, n_blk):
        parts = [segment(qq_ref, s_ref, range(c0, min(c0 + ATTN_SEG, n_blk)), n_blk - 1)
                 for c0 in range(0, n_blk, ATTN_SEG)]
        m = parts[0][0]
        for mp, _, _ in parts[1:]:
            m = jnp.maximum(m, mp)
        ls = None
        pv = None
        for mp, lsp, pvp in parts:
            w = jnp.exp(mp - m)
            ls = w * lsp if ls is None else ls + w * lsp
            pv = w * pvp if pv is None else pv + w * pvp
        accn = pv / jnp.sum(ls, axis=-1, keepdims=True)
        lam = _lam(lq1, lk1, lq2, lk2, lam_init)
        o = accn[0:tq, :] - lam * accn[tq:2 * tq, :]
        return _subln(o, sg_ref[...], lam_init).astype(BF16)

    def paged_update(kbs, vbs, bias):
        qs = qs_ref[...]
        ss = [lax.dot_general(qs, kb, (((1,), (1,)), ((), ())), preferred_element_type=F32) + bias
              for kb in kbs]
        mx = _fold_lanes(ss[0], jnp.maximum)
        for s in ss[1:]:
            mx = jnp.maximum(mx, _fold_lanes(s, jnp.maximum))
        m_prev = m_ref[...]
        m_new = jnp.maximum(m_prev, jnp.max(mx, axis=-1, keepdims=True))
        alpha = jnp.exp(m_prev - m_new)
        ls = None
        pv = None
        for s, vb in zip(ss, vbs):
            p = jnp.exp(s - m_new)
            f = _fold_lanes(p, jnp.add)
            ls = f if ls is None else ls + f
            d = jnp.dot(p.astype(BF16), vb, preferred_element_type=F32)
            pv = d if pv is None else pv + d
        l_ref[...] = alpha * l_ref[...] + jnp.sum(ls, axis=-1, keepdims=True)
        acc_ref[...] = alpha * acc_ref[...] + pv
        m_ref[...] = m_new

    def paged(step):
        if step == 0:
            m_ref[...] = jnp.full(m_ref.shape, NEG, F32)
            l_ref[...] = jnp.zeros(l_ref.shape, F32)
            acc_ref[...] = jnp.zeros(acc_ref.shape, F32)
            paged_update([kn_ref[...]], [vn_ref[...]], nbias_ref[...])
        paged_update([r[...].astype(BF16) for r in pk_refs],
                     [r[...].astype(BF16) for r in pv_refs], bias_ref[...])
        if step == nq // 2 - 1:
            accn = acc_ref[...] / l_ref[...]
            half = accn.shape[0] // 2
            lam = _lam(lq1, lk1, lq2, lk2, lam_init)
            o = accn[0:half, :] - lam * accn[half:, :]
            os_ref[...] = _subln(o, sg_ref[...], lam_init).astype(BF16)

    def variant(step):
        paged(step)
        o_ref[0:tq, :] = attend(qq_lo, s_lo, step + 1)
        o_ref[tq:2 * tq, :] = attend(qq_hi, s_hi, nq - step)

    for n in range(nq // 2):
        pl.when(i == n)(functools.partial(variant, n))


def _paired_block(r, nq):
    b, rb = r // nq, r % nq
    return b * nq + jnp.where(rb < nq // 2, 2 * rb, 2 * (nq - 1 - rb) + 1)


def _attention(q_bf, k_bf, v_bf, q_s, kn_bf, vn_bf, cache_k, cache_v, page_table, lams, subln_g,
               batch, seq, dec_batch, dec_seq, lam_init, tq):
    m, w = q_bf.shape
    n_heads = w // HEAD_W
    nq = seq // tq
    half = nq // 2
    n_pool, page, _, _ = cache_k.shape
    n_pages = page_table.shape[1]
    n_pp = n_pages // half
    assert nq % 2 == 0 and n_pp * half == n_pages and batch * n_heads == dec_batch
    rows_q = dec_seq * n_heads
    page_rows = page * n_heads
    q4 = q_s.reshape(dec_batch, dec_seq, n_heads, HEAD_W).transpose(0, 2, 1, 3)
    q4 = q4.reshape(dec_batch, rows_q, HEAD_W)
    lane = jnp.arange(HEAD_W)
    qt = jnp.concatenate([jnp.where(lane < QK_DIM, q4, 0), jnp.where(lane >= QK_DIM, q4, 0)],
                         axis=1).astype(BF16)
    pad = lambda a: jnp.pad(a.reshape(dec_batch, rows_q, HEAD_W),
                            ((0, 0), (0, LANES - rows_q), (0, 0)))
    kn, vn = pad(kn_bf), pad(vn_bf)
    c = np.arange(2 * rows_q)
    c_head = (c % rows_q) // dec_seq
    c_query = c % dec_seq
    r = np.arange(page_rows)
    bias = jnp.asarray(np.where((r % n_heads)[None, :] == c_head[:, None], 0.0, NEG), F32)
    rn = np.arange(LANES)
    ok = ((rn % n_heads)[None, :] == c_head[:, None]) & ((rn // n_heads)[None, :] <= c_query[:, None]) \
        & (rn < rows_q)[None, :]
    nbias = jnp.asarray(np.where(ok, 0.0, NEG), F32)
    ck = cache_k.reshape(n_pool, page_rows, HEAD_W)
    cv = cache_v.reshape(n_pool, page_rows, HEAD_W)

    small = lambda b, h, i, pt: (0, 0)
    kv = lambda b, h, i, pt: (b, h)
    per_entry = lambda b, h, i, pt: (b * n_heads + h, 0, 0)

    def page_spec(t):
        return pl.BlockSpec((None, page_rows, HEAD_W),
                            lambda b, h, i, pt: (pt[b * n_heads + h, i * n_pp + t], 0, 0))

    grid_spec = pltpu.PrefetchScalarGridSpec(
        num_scalar_prefetch=1,
        grid=(batch, n_heads, half),
        in_specs=[pl.BlockSpec((1, QK_DIM), small)] * 4
                 + [pl.BlockSpec((1, HEAD_W), small),
                    pl.BlockSpec((tq, HEAD_W), lambda b, h, i, pt: (b * nq + i, h)),
                    pl.BlockSpec((tq, HEAD_W), lambda b, h, i, pt: (b * nq + nq - 1 - i, h)),
                    pl.BlockSpec((seq, HEAD_W), kv),
                    pl.BlockSpec((seq, HEAD_W), kv),
                    pl.BlockSpec((None, 2 * rows_q, HEAD_W), per_entry),
                    pl.BlockSpec((2 * rows_q, page_rows), small),
                    pl.BlockSpec((2 * rows_q, LANES), small),
                    pl.BlockSpec((None, LANES, HEAD_W), per_entry),
                    pl.BlockSpec((None, LANES, HEAD_W), per_entry)]
                 + [page_spec(t) for t in range(n_pp)] * 2,
        out_specs=[pl.BlockSpec((2 * tq, HEAD_W), lambda b, h, i, pt: (b * half + i, h)),
                   pl.BlockSpec((None, rows_q, HEAD_W), per_entry)],
        scratch_shapes=[pltpu.VMEM((2 * tq, HEAD_W), BF16),
                        pltpu.VMEM((2 * tq, HEAD_W), BF16),
                        pltpu.VMEM((2 * tq, half * tq), F32),
                        pltpu.VMEM((2 * tq, seq), F32),
                        pltpu.VMEM((2 * rows_q, 1), F32),
                        pltpu.VMEM((2 * rows_q, 1), F32),
                        pltpu.VMEM((2 * rows_q, HEAD_W), F32)],
    )
    o_p, o_s = pl.pallas_call(
        functools.partial(_attn_kernel, tq=tq, nq=nq, n_pp=n_pp, lam_init=lam_init),
        grid_spec=grid_spec,
        out_shape=[jax.ShapeDtypeStruct((m, w), BF16),
                   jax.ShapeDtypeStruct((dec_batch, rows_q, HEAD_W), BF16)],
        compiler_params=_cparams(("arbitrary", "arbitrary", "arbitrary")),
        name="attention",
    )(page_table, *lams, subln_g, q_bf, q_bf, k_bf, v_bf, qt, bias, nbias, kn, vn,
      *([ck] * n_pp), *([cv] * n_pp))
    o_s = o_s.reshape(dec_batch, n_heads, dec_seq, HEAD_W).transpose(0, 2, 1, 3)
    return o_p, o_s.reshape(dec_batch * dec_seq, n_heads * HEAD_W)


def _ln_swish(y, g, b):
    mu = jnp.mean(y, axis=-1, keepdims=True)
    yc = y - mu
    var = jnp.mean(yc * yc, axis=-1, keepdims=True)
    z = yc * lax.rsqrt(var + EPS) * g + b
    return z * (1.0 / (1.0 + jnp.exp(-z)))


CONV_HALO = 32
CONV_CHUNK = 16


def _conv_prompt_kernel(cur_ref, tail_ref, w_ref, cb_ref, g_ref, b_ref, o_ref, ext_ref, y_ref, *, tt):
    t = pl.program_id(1)
    n_slab = cur_ref.shape[1] // LANES
    for sl in range(n_slab):
        lanes = slice(sl * LANES, (sl + 1) * LANES)
        tail = tail_ref[:, lanes]
        ext_ref[sl, 0:CONV_HALO, :] = jnp.where(t == 0, jnp.zeros_like(tail), tail)
        ext_ref[sl, CONV_HALO:CONV_HALO + tt, :] = cur_ref[:, lanes]
    off = CONV_HALO - (CONV_K - 1)
    n_ch = cur_ref.shape[1]

    def group(i, carry):
        base = i * CONV_CHUNK
        half = CONV_CHUNK // 2
        for sl in range(n_slab):
            lanes = slice(sl * LANES, (sl + 1) * LANES)
            xs = [ext_ref[sl, pl.ds(base + (off + k), half, stride=2), :] for k in range(CONV_K + 1)]
            even = jnp.zeros((half, LANES), F32)
            odd = jnp.zeros((half, LANES), F32)
            for j in range(CONV_K):
                w = w_ref[j:j + 1, lanes]
                even = even + xs[j] * w
                odd = odd + xs[j + 1] * w
            y_ref[sl, pl.ds(base, half, stride=2), :] = even + cb_ref[:, lanes]
            y_ref[sl, pl.ds(base + 1, half, stride=2), :] = odd + cb_ref[:, lanes]
        return carry

    lax.fori_loop(0, tt // CONV_CHUNK, group, 0)

    tot = y_ref[0]
    for sl in range(1, n_slab):
        tot = tot + y_ref[sl]
    mu = jnp.sum(tot, axis=-1, keepdims=True) * (1.0 / n_ch)
    sq = None
    for sl in range(n_slab):
        yc = y_ref[sl] - mu
        sq = yc * yc if sq is None else sq + yc * yc
    rstd = lax.rsqrt(jnp.sum(sq, axis=-1, keepdims=True) * (1.0 / n_ch) + EPS)
    for sl in range(n_slab):
        lanes = slice(sl * LANES, (sl + 1) * LANES)
        z = (y_ref[sl] - mu) * rstd * g_ref[:, lanes] + b_ref[:, lanes]
        o_ref[:, lanes] = (z * (1.0 / (1.0 + jnp.exp(-z)))).astype(BF16)


def _conv_prompt(u, conv_w, conv_b, ln_g, ln_b, batch, seq, tt=256):
    m, c = u.shape
    nt = seq // tt
    ratio = tt // CONV_HALO
    cur = lambda b, t: (b * nt + t, 0)
    tail = lambda b, t: (jnp.maximum((b * nt + t) * ratio - 1, 0), 0)
    small = lambda b, t: (0, 0)
    return pl.pallas_call(
        functools.partial(_conv_prompt_kernel, tt=tt),
        grid=(batch, nt),
        in_specs=[pl.BlockSpec((tt, c), cur),
                  pl.BlockSpec((CONV_HALO, c), tail),
                  pl.BlockSpec((CONV_K, c), small),
                  pl.BlockSpec((1, c), small),
                  pl.BlockSpec((1, c), small),
                  pl.BlockSpec((1, c), small)],
        out_specs=pl.BlockSpec((tt, c), cur),
        out_shape=jax.ShapeDtypeStruct((m, c), BF16),
        scratch_shapes=[pltpu.VMEM((c // LANES, CONV_HALO + tt, LANES), F32),
                        pltpu.VMEM((c // LANES, tt, LANES), F32)],
        compiler_params=_cparams(("parallel", "parallel")),
        name="conv_prompt",
    )(u, u, conv_w, conv_b, ln_g, ln_b)


def _conv_sample_kernel(ext_ref, w_ref, cb_ref, g_ref, b_ref, o_ref):
    nb, rows, _ = o_ref.shape
    for e in range(nb):
        acc = jnp.zeros(o_ref.shape[1:], F32)
        for j in range(CONV_K):
            acc = acc + ext_ref[e, j:j + rows, :] * w_ref[j:j + 1, :]
        y = acc + cb_ref[...]
        o_ref[e] = _ln_swish(y, g_ref[...], b_ref[...]).astype(BF16)


CONV_SAMPLE_ROWS = 8
CONV_SAMPLE_ENTRIES = 8


def _conv_sample(ext, conv_w, conv_b, ln_g, ln_b, dec_seq):
    nb, n_rows, c = ext.shape
    assert dec_seq <= CONV_SAMPLE_ROWS and nb % CONV_SAMPLE_ENTRIES == 0
    pad_rows = (CONV_K - 1) + CONV_SAMPLE_ROWS
    ext_p = jnp.pad(ext, ((0, 0), (0, pad_rows - n_rows), (0, 0)))
    small = lambda b: (0, 0)
    o = pl.pallas_call(
        _conv_sample_kernel,
        grid=(nb // CONV_SAMPLE_ENTRIES,),
        in_specs=[pl.BlockSpec((CONV_SAMPLE_ENTRIES, pad_rows, c), lambda b: (b, 0, 0)),
                  pl.BlockSpec((CONV_K, c), small),
                  pl.BlockSpec((1, c), small),
                  pl.BlockSpec((1, c), small),
                  pl.BlockSpec((1, c), small)],
        out_specs=pl.BlockSpec((CONV_SAMPLE_ENTRIES, CONV_SAMPLE_ROWS, c), lambda b: (b, 0, 0)),
        out_shape=jax.ShapeDtypeStruct((nb, CONV_SAMPLE_ROWS, c), BF16),
        compiler_params=_cparams(("parallel",)),
        name="conv_sample",
    )(ext_p, conv_w, conv_b, ln_g, ln_b)
    return o[:, :dec_seq].reshape(nb * dec_seq, c)


def _outproj_kernel(o_ref, c_ref, x_ref, wa_ref, wc_ref, g_ref, h_ref, hn_ref):
    h = x_ref[...] + jnp.dot(o_ref[...], wa_ref[...], preferred_element_type=F32) \
        + jnp.dot(c_ref[...], wc_ref[...], preferred_element_type=F32)
    h_ref[...] = h
    ms = jnp.mean(h * h, axis=-1, keepdims=True)
    hn_ref[...] = (h * lax.rsqrt(ms + EPS) * g_ref[...]).astype(BF16)


def _outproj(o_n, c, x, w_out_bf, g2, tm, o_block=None):
    m, d = x.shape
    wa = o_n.shape[1]
    row = lambda i: (i, 0)
    o_row = row if o_block is None else (lambda i: (o_block(i), 0))
    return pl.pallas_call(
        _outproj_kernel,
        grid=(m // tm,),
        in_specs=[pl.BlockSpec((tm, wa), o_row),
                  pl.BlockSpec((tm, c.shape[1]), row),
                  pl.BlockSpec((tm, d), row),
                  pl.BlockSpec((wa, d), lambda i: (0, 0)),
                  pl.BlockSpec((c.shape[1], d), lambda i: (wa // c.shape[1], 0)),
                  pl.BlockSpec((1, d), lambda i: (0, 0))],
        out_specs=[pl.BlockSpec((tm, d), row), pl.BlockSpec((tm, d), row)],
        out_shape=[jax.ShapeDtypeStruct((m, d), F32), jax.ShapeDtypeStruct((m, d), BF16)],
        compiler_params=_cparams(("parallel",)),
        name="outproj",
    )(o_n, c, x, w_out_bf, w_out_bf, g2)


def _ffn_kernel(hn_ref, h_ref, wu_ref, wd_ref, g_ref, y_ref, acc_ref):
    f = pl.program_id(1)

    @pl.when(f == 0)
    def _():
        acc_ref[...] = jnp.zeros(acc_ref.shape, F32)

    a = jnp.maximum(jnp.dot(hn_ref[...], wu_ref[...], preferred_element_type=F32), 0.0)
    acc_ref[...] += jnp.dot((a * a).astype(BF16), wd_ref[...], preferred_element_type=F32)

    @pl.when(f == pl.num_programs(1) - 1)
    def _():
        y = h_ref[...] + acc_ref[...]
        ms = jnp.mean(y * y, axis=-1, keepdims=True)
        y_ref[...] = y * lax.rsqrt(ms + EPS) * g_ref[...]


def _ffn(hn, h, w_up_bf, w_down_bf, final_g, tm, tf):
    m, d = h.shape
    dff = w_up_bf.shape[1]
    row = lambda i, f: (i, 0)
    return pl.pallas_call(
        _ffn_kernel,
        grid=(m // tm, dff // tf),
        in_specs=[pl.BlockSpec((tm, d), row),
                  pl.BlockSpec((tm, d), row),
                  pl.BlockSpec((d, tf), lambda i, f: (0, f)),
                  pl.BlockSpec((tf, d), lambda i, f: (f, 0)),
                  pl.BlockSpec((1, d), lambda i, f: (0, 0))],
        out_specs=pl.BlockSpec((tm, d), row),
        out_shape=jax.ShapeDtypeStruct((m, d), F32),
        scratch_shapes=[pltpu.VMEM((tm, d), F32)],
        compiler_params=_cparams(("parallel", "arbitrary")),
        name="ffn",
    )(hn, h, w_up_bf, w_down_bf, final_g)


def kernel(x_prompt, x_sample, cache_k, cache_v, state_conv, page_table, norm1_g, w_in,
           lambda_q1, lambda_k1, lambda_q2, lambda_k2, subln_g, conv_w, conv_b,
           conv_ln_g, conv_ln_b, w_out, norm2_g, w_up, w_down, final_g):
    depth = w_in.shape[0]
    assert depth == 1, "final norm is fused into the (single) layer's FFN kernel"
    batch, seq, d = x_prompt.shape
    dec_batch, dec_seq, _ = x_sample.shape
    n_heads, head_w = cache_k.shape[3], cache_k.shape[4]
    assert head_w == HEAD_W
    past = page_table.shape[1] * cache_k.shape[2]
    conv_c = conv_w.shape[2]

    xp = x_prompt.reshape(batch * seq, d)
    xs = x_sample.reshape(dec_batch * dec_seq, d)
    tm_p = 512
    tm_s = dec_batch * dec_seq

    tabs_p = _rope_tables(np.arange(seq))
    tabs_s = _rope_tables(np.tile(past + np.arange(dec_seq), dec_batch))

    l = 0
    lam_init = 0.8 - 0.6 * math.exp(-0.3 * l)
    row = lambda a: a[l].reshape(1, -1)
    lams = (row(lambda_q1), row(lambda_k1), row(lambda_q2), row(lambda_k2))
    g1, g2, sg = row(norm1_g), row(norm2_g), row(subln_g)
    cb, lg, lb = row(conv_b), row(conv_ln_g), row(conv_ln_b)
    fg = final_g.reshape(1, -1)
    w_in_bf = w_in[l].astype(BF16)
    w_out_bf = w_out[l].astype(BF16)
    w_up_bf = w_up[l].astype(BF16)
    w_down_bf = w_down[l].astype(BF16)
    cw = conv_w[l]

    q_p, k_p, kb_p, v_p, vb_p, u_p = _inproj(xp, g1, w_in_bf, tabs_p, tm_p, seq // tm_p)
    q_s, k_s, kb_s, v_s, vb_s, u_s = _inproj(xs, g1, w_in_bf, tabs_s, tm_s, 1)
    tq = 256
    o_p, o_s = _attention(q_p, kb_p, vb_p, q_s, kb_s, vb_s, cache_k[l], cache_v[l], page_table,
                          lams, sg, batch, seq, dec_batch, dec_seq, lam_init, tq)

    c_p = _conv_prompt(u_p, cw, cb, lg, lb, batch, seq)
    h_p, hn_p = _outproj(o_p, c_p, xp, w_out_bf, g2, tq,
                         o_block=functools.partial(_paired_block, nq=seq // tq))
    y_p = _ffn(hn_p, h_p, w_up_bf, w_down_bf, fg, tm_p, 1024)

    ext_s = jnp.concatenate([state_conv[l], u_s.reshape(dec_batch, dec_seq, conv_c)], axis=1)
    c_s = _conv_sample(ext_s, cw, cb, lg, lb, dec_seq)
    h_s, hn_s = _outproj(o_s, c_s, xs, w_out_bf, g2, tm_s)
    y_s = _ffn(hn_s, h_s, w_up_bf, w_down_bf, fg, tm_s, 1024)

    keep = CONV_K - 1
    return (y_p.reshape(batch, seq, d),
            y_s.reshape(dec_batch, dec_seq, d),
            k_p.reshape(1, batch, seq, n_heads, head_w),
            v_p.reshape(1, batch, seq, n_heads, head_w),
            u_p.reshape(batch, seq, conv_c)[:, seq - keep:][None],
            k_s.reshape(1, dec_batch, dec_seq, n_heads, head_w),
            v_s.reshape(1, dec_batch, dec_seq, n_heads, head_w),
            ext_s[:, dec_seq:][None])
```

```python
import functools
import math

import numpy as np
import jax
import jax.numpy as jnp
from jax import lax
from jax.experimental import pallas as pl
from jax.experimental.pallas import tpu as pltpu

F32 = jnp.float32
BF16 = jnp.bfloat16

EPS = 1e-6
ROPE_THETA = 500000.0
LANES = 128
HEAD_W = 128
QK_DIM = 64
ROPE_DIM = 16
CONV_K = 31
NEG = -1e30
VMEM_LIMIT = 56 * 1024 * 1024


def _cparams(sem):
    return pltpu.CompilerParams(dimension_semantics=sem, vmem_limit_bytes=VMEM_LIMIT)


def _rope(z, c, s1, s2):
    return z * c + pltpu.roll(z, LANES - 8, 1) * s1 + pltpu.roll(z, 8, 1) * s2


def _inproj_kernel(x_ref, g_ref, w_ref, c_ref, s1_ref, s2_ref,
                   q_ref, k_ref, kb_ref, v_ref, vb_ref, u_ref, xn_ref, za_ref, zb_ref):
    j = pl.program_id(1)
    n_heads = za_ref.shape[1] // HEAD_W

    def matmul():
        return jnp.dot(xn_ref[...], w_ref[...], preferred_element_type=F32)

    def roped(z_ref):
        c, s1, s2 = c_ref[...], s1_ref[...], s2_ref[...]
        for h in range(n_heads):
            sl = slice(h * HEAD_W, (h + 1) * HEAD_W)
            yield sl, _rope(z_ref[:, sl], c, s1, s2)

    @pl.when(j == 0)
    def _():
        x = x_ref[...]
        ms = jnp.mean(x * x, axis=-1, keepdims=True)
        xn_ref[...] = (x * lax.rsqrt(ms + EPS) * g_ref[...]).astype(BF16)
        za_ref[...] = matmul()

    @pl.when(j == 1)
    def _():
        zb_ref[...] = matmul()
        for sl, qr in roped(za_ref):
            q_ref[:, sl] = (qr * (QK_DIM ** -0.5)).astype(BF16)

    @pl.when(j == 2)
    def _():
        za_ref[...] = matmul()
        for sl, kr in roped(zb_ref):
            k_ref[:, sl] = kr
            kb_ref[:, sl] = kr.astype(BF16)

    @pl.when(j == 3)
    def _():
        zb_ref[...] = matmul()
        v = za_ref[...]
        v_ref[...] = v
        vb_ref[...] = v.astype(BF16)

    @pl.when(j == 4)
    def _():
        u_ref[...] = zb_ref[...] * (1.0 / (1.0 + jnp.exp(-matmul())))


def _inproj(x, g, w_bf, tabs, tm, n_pos_blocks):
    m, d = x.shape
    nw = w_bf.shape[1] // 5
    row = lambda i, j: (i, 0)
    tab = lambda i, j: (i % n_pos_blocks, 0)
    out_f = jax.ShapeDtypeStruct((m, nw), F32)
    out_b = jax.ShapeDtypeStruct((m, nw), BF16)
    return pl.pallas_call(
        _inproj_kernel,
        grid=(m // tm, 5),
        in_specs=[pl.BlockSpec((tm, d), row),
                  pl.BlockSpec((1, d), lambda i, j: (0, 0)),
                  pl.BlockSpec((d, nw), lambda i, j: (0, j)),
                  pl.BlockSpec((tm, LANES), tab),
                  pl.BlockSpec((tm, LANES), tab),
                  pl.BlockSpec((tm, LANES), tab)],
        out_specs=[pl.BlockSpec((tm, nw), row)] * 6,
        out_shape=[out_b, out_f, out_b, out_f, out_b, out_f],
        scratch_shapes=[pltpu.VMEM((tm, d), BF16), pltpu.VMEM((tm, nw), F32),
                        pltpu.VMEM((tm, nw), F32)],
        compiler_params=_cparams(("parallel", "arbitrary")),
        name="inproj",
    )(x, g, w_bf, *tabs)


def _rope_tables(pos):
    pos = np.asarray(pos, np.float64)
    half = ROPE_DIM // 2
    inv = np.power(ROPE_THETA, -np.arange(half, dtype=np.float64) * 2.0 / ROPE_DIM)
    ang = pos[:, None] * inv[None, :]
    cos, sin = np.cos(ang), np.sin(ang)
    t = pos.shape[0]
    one = np.ones((t, QK_DIM - ROPE_DIM))
    zero = np.zeros((t, QK_DIM - ROPE_DIM))
    zh = np.zeros((t, half))
    c = np.concatenate([cos, cos, one], axis=-1)
    s1 = np.concatenate([-sin, zh, zero], axis=-1)
    s2 = np.concatenate([zh, sin, zero], axis=-1)
    dup = lambda a: jnp.asarray(np.concatenate([a, a], axis=-1), F32)
    return dup(c), dup(s1), dup(s2)


def _lam(lq1, lk1, lq2, lk2, lam_init):
    a = jnp.sum(lq1[...] * lk1[...], axis=-1, keepdims=True)
    b = jnp.sum(lq2[...] * lk2[...], axis=-1, keepdims=True)
    return jnp.exp(a) - jnp.exp(b) + lam_init


def _subln(o, g, lam_init):
    ms = jnp.mean(o * o, axis=-1, keepdims=True)
    return (o * lax.rsqrt(ms + EPS) * g) * (1.0 - lam_init)


def _fold_lanes(x, op):
    f = x[:, 0:LANES]
    for t in range(1, x.shape[1] // LANES):
        f = op(f, x[:, t * LANES:(t + 1) * LANES])
    return f


ATTN_SEG = 3


def _attn_kernel(pt_ref, lq1, lk1, lq2, lk2, sg_ref, qlo_ref, qhi_ref, k_ref, v_ref,
                 qs_ref, bias_ref, nbias_ref, kn_ref, vn_ref, *rest, tq, nq, n_pp, lam_init):
    del pt_ref
    pk_refs = rest[:n_pp]
    pv_refs = rest[n_pp:2 * n_pp]
    o_ref, os_ref = rest[2 * n_pp:2 * n_pp + 2]
    qq_lo, qq_hi, s_lo, s_hi, m_ref, l_ref, acc_ref = rest[2 * n_pp + 2:]
    i = pl.program_id(2)
    lane = lax.broadcasted_iota(jnp.int32, (tq, HEAD_W), 1)
    for q_ref, qq_ref in ((qlo_ref, qq_lo), (qhi_ref, qq_hi)):
        q = q_ref[...].astype(F32)
        qq_ref[0:tq, :] = jnp.where(lane < QK_DIM, q, 0.0).astype(BF16)
        qq_ref[tq:2 * tq, :] = jnp.where(lane >= QK_DIM, q, 0.0).astype(BF16)

    def segment(qq_ref, s_ref, blocks, diag):
        mx = None
        for c in blocks:
            cols = slice(c * tq, (c + 1) * tq)
            s = lax.dot_general(qq_ref[...], k_ref[cols, :], (((1,), (1,)), ((), ())),
                                preferred_element_type=F32)
            if c == diag:
                r = lax.broadcasted_iota(jnp.int32, s.shape, 0)
                col = lax.broadcasted_iota(jnp.int32, s.shape, 1)
                r = jnp.where(r >= tq, r - tq, r)
                s = jnp.where(col <= r, s, NEG)
            s_ref[:, cols] = s
            f = _fold_lanes(s, jnp.maximum)
            mx = f if mx is None else jnp.maximum(mx, f)
        m = jnp.max(mx, axis=-1, keepdims=True)
        ls = None
        pv = None
        for c in blocks:
            cols = slice(c * tq, (c + 1) * tq)
            p = jnp.exp(s_ref[:, cols] - m)
            f = _fold_lanes(p, jnp.add)
            ls = f if ls is None else ls + f
            d = jnp.dot(p.astype(BF16), v_ref[cols, :], preferred_element_type=F32)
            pv = d if pv is None else pv + d
        return m, ls, pv

    def attend(qq_ref, s_ref, n_blk):
        parts = [segment(qq_ref, s_ref, range(c0, min(c0 + ATTN_SEG, n_blk)), n_blk - 1)
                 for c0 in range(0, n_blk, ATTN_SEG)]
        m = parts[0][0]
        for mp, _, _ in parts[1:]:
            m = jnp.maximum(m, mp)
        ls = None
        pv = None
        for mp, lsp, pvp in parts:
            w = jnp.exp(mp - m)
            ls = w * lsp if ls is None else ls + w * lsp
            pv = w * pvp if pv is None else pv + w * pvp
        accn = pv / jnp.sum(ls, axis=-1, keepdims=True)
        lam = _lam(lq1, lk1, lq2, lk2, lam_init)
        o = accn[0:tq, :] - lam * accn[tq:2 * tq, :]
        return _subln(o, sg_ref[...], lam_init).astype(BF16)

    def paged_update(kbs, vbs, bias):
        qs = qs_ref[...]
        ss = [lax.dot_general(qs, kb, (((1,), (1,)), ((), ())), preferred_element_type=F32) + bias
              for kb in kbs]
        mx = _fold_lanes(ss[0], jnp.maximum)
        for s in ss[1:]:
            mx = jnp.maximum(mx, _fold_lanes(s, jnp.maximum))
        m_prev = m_ref[...]
        m_new = jnp.maximum(m_prev, jnp.max(mx, axis=-1, keepdims=True))
        alpha = jnp.exp(m_prev - m_new)
        ls = None
        pv = None
        for s, vb in zip(ss, vbs):
            p = jnp.exp(s - m_new)
            f = _fold_lanes(p, jnp.add)
            ls = f if ls is None else ls + f
            d = jnp.dot(p.astype(BF16), vb, preferred_element_type=F32)
            pv = d if pv is None else pv + d
        l_ref[...] = alpha * l_ref[...] + jnp.sum(ls, axis=-1, keepdims=True)
        acc_ref[...] = alpha * acc_ref[...] + pv
        m_ref[...] = m_new

    def paged(step):
        if step == 0:
            m_ref[...] = jnp.full(m_ref.shape, NEG, F32)
            l_ref[...] = jnp.zeros(l_ref.shape, F32)
            acc_ref[...] = jnp.zeros(acc_ref.shape, F32)
            paged_update([kn_ref[...]], [vn_ref[...]], nbias_ref[...])
        paged_update([r[...].astype(BF16) for r in pk_refs],
                     [r[...].astype(BF16) for r in pv_refs], bias_ref[...])
        if step == nq // 2 - 1:
            accn = acc_ref[...] / l_ref[...]
            half = accn.shape[0] // 2
            lam = _lam(lq1, lk1, lq2, lk2, lam_init)
            o = accn[0:half, :] - lam * accn[half:, :]
            os_ref[...] = _subln(o, sg_ref[...], lam_init).astype(BF16)

    def variant(step):
        paged(step)
        o_ref[0:tq, :] = attend(qq_lo, s_lo, step + 1)
        o_ref[tq:2 * tq, :] = attend(qq_hi, s_hi, nq - step)

    for n in range(nq // 2):
        pl.when(i == n)(functools.partial(variant, n))


def _paired_block(r, nq):
    b, rb = r // nq, r % nq
    return b * nq + jnp.where(rb < nq // 2, 2 * rb, 2 * (nq - 1 - rb) + 1)


def _attention(q_bf, k_bf, v_bf, q_s, kn_bf, vn_bf, cache_k, cache_v, page_table, lams, subln_g,
               batch, seq, dec_batch, dec_seq, lam_init, tq):
    m, w = q_bf.shape
    n_heads = w // HEAD_W
    nq = seq // tq
    half = nq // 2
    n_pool, page, _, _ = cache_k.shape
    n_pages = page_table.shape[1]
    n_pp = n_pages // half
    assert nq % 2 == 0 and n_pp * half == n_pages and batch * n_heads == dec_batch
    rows_q = dec_seq * n_heads
    page_rows = page * n_heads
    q4 = q_s.reshape(dec_batch, dec_seq, n_heads, HEAD_W).transpose(0, 2, 1, 3)
    q4 = q4.reshape(dec_batch, rows_q, HEAD_W)
    lane = jnp.arange(HEAD_W)
    qt = jnp.concatenate([jnp.where(lane < QK_DIM, q4, 0), jnp.where(lane >= QK_DIM, q4, 0)],
                         axis=1).astype(BF16)
    pad = lambda a: jnp.pad(a.reshape(dec_batch, rows_q, HEAD_W),
                            ((0, 0), (0, LANES - rows_q), (0, 0)))
    kn, vn = pad(kn_bf), pad(vn_bf)
    c = np.arange(2 * rows_q)
    c_head = (c % rows_q) // dec_seq
    c_query = c % dec_seq
    r = np.arange(page_rows)
    bias = jnp.asarray(np.where((r % n_heads)[None, :] == c_head[:, None], 0.0, NEG), F32)
    rn = np.arange(LANES)
    ok = ((rn % n_heads)[None, :] == c_head[:, None]) & ((rn // n_heads)[None, :] <= c_query[:, None]) \
        & (rn < rows_q)[None, :]
    nbias = jnp.asarray(np.where(ok, 0.0, NEG), F32)
    ck = cache_k.reshape(n_pool, page_rows, HEAD_W)
    cv = cache_v.reshape(n_pool, page_rows, HEAD_W)

    small = lambda b, h, i, pt: (0, 0)
    kv = lambda b, h, i, pt: (b, h)
    per_entry = lambda b, h, i, pt: (b * n_heads + h, 0, 0)

    def page_spec(t):
        return pl.BlockSpec((None, page_rows, HEAD_W),
                            lambda b, h, i, pt: (pt[b * n_heads + h, i * n_pp + t], 0, 0))

    grid_spec = pltpu.PrefetchScalarGridSpec(
        num_scalar_prefetch=1,
        grid=(batch, n_heads, half),
        in_specs=[pl.BlockSpec((1, QK_DIM), small)] * 4
                 + [pl.BlockSpec((1, HEAD_W), small),
                    pl.BlockSpec((tq, HEAD_W), lambda b, h, i, pt: (b * nq + i, h)),
                    pl.BlockSpec((tq, HEAD_W), lambda b, h, i, pt: (b * nq + nq - 1 - i, h)),
                    pl.BlockSpec((seq, HEAD_W), kv),
                    pl.BlockSpec((seq, HEAD_W), kv),
                    pl.BlockSpec((None, 2 * rows_q, HEAD_W), per_entry),
                    pl.BlockSpec((2 * rows_q, page_rows), small),
                    pl.BlockSpec((2 * rows_q, LANES), small),
                    pl.BlockSpec((None, LANES, HEAD_W), per_entry),
                    pl.BlockSpec((None, LANES, HEAD_W), per_entry)]
                 + [page_spec(t) for t in range(n_pp)] * 2,
        out_specs=[pl.BlockSpec((2 * tq, HEAD_W), lambda b, h, i, pt: (b * half + i, h)),
                   pl.BlockSpec((None, rows_q, HEAD_W), per_entry)],
        scratch_shapes=[pltpu.VMEM((2 * tq, HEAD_W), BF16),
                        pltpu.VMEM((2 * tq, HEAD_W), BF16),
                        pltpu.VMEM((2 * tq, half * tq), F32),
                        pltpu.VMEM((2 * tq, seq), F32),
                        pltpu.VMEM((2 * rows_q, 1), F32),
                        pltpu.VMEM((2 * rows_q, 1), F32),
                        pltpu.VMEM((2 * rows_q, HEAD_W), F32)],
    )
    o_p, o_s = pl.pallas_call(
        functools.partial(_attn_kernel, tq=tq, nq=nq, n_pp=n_pp, lam_init=lam_init),
        grid_spec=grid_spec,
        out_shape=[jax.ShapeDtypeStruct((m, w), BF16),
                   jax.ShapeDtypeStruct((dec_batch, rows_q, HEAD_W), BF16)],
        compiler_params=_cparams(("arbitrary", "arbitrary", "arbitrary")),
        name="attention",
    )(page_table, *lams, subln_g, q_bf, q_bf, k_bf, v_bf, qt, bias, nbias, kn, vn,
      *([ck] * n_pp), *([cv] * n_pp))
    o_s = o_s.reshape(dec_batch, n_heads, dec_seq, HEAD_W).transpose(0, 2, 1, 3)
    return o_p, o_s.reshape(dec_batch * dec_seq, n_heads * HEAD_W)


def _ln_swish(y, g, b):
    mu = jnp.mean(y, axis=-1, keepdims=True)
    yc = y - mu
    var = jnp.mean(yc * yc, axis=-1, keepdims=True)
    z = yc * lax.rsqrt(var + EPS) * g + b
    return z * (1.0 / (1.0 + jnp.exp(-z)))


CONV_HALO = 32
CONV_CHUNK = 16


def _conv_block(t, cur_ref, tail_ref, w_ref, cb_ref, g_ref, b_ref, ext_ref, y_ref, tt):
    n_slab = cur_ref.shape[1] // LANES
    for sl in range(n_slab):
        lanes = slice(sl * LANES, (sl + 1) * LANES)
        tail = tail_ref[:, lanes]
        ext_ref[sl, 0:CONV_HALO, :] = jnp.where(t == 0, jnp.zeros_like(tail), tail)
        ext_ref[sl, CONV_HALO:CONV_HALO + tt, :] = cur_ref[:, lanes]
    off = CONV_HALO - (CONV_K - 1)
    n_ch = cur_ref.shape[1]

    half = CONV_CHUNK // 2
    for base in range(0, tt, CONV_CHUNK):
        for sl in range(n_slab):
            lanes = slice(sl * LANES, (sl + 1) * LANES)
            xs = [ext_ref[sl, pl.ds(base + off + k, half, stride=2), :] for k in range(CONV_K + 1)]
            even = jnp.zeros((half, LANES), F32)
            odd = jnp.zeros((half, LANES), F32)
            for j in range(CONV_K):
                w = w_ref[j:j + 1, lanes]
                even = even + xs[j] * w
                odd = odd + xs[j + 1] * w
            y_ref[sl, pl.ds(base, half, stride=2), :] = even + cb_ref[:, lanes]
            y_ref[sl, pl.ds(base + 1, half, stride=2), :] = odd + cb_ref[:, lanes]

    tot = y_ref[0]
    for sl in range(1, n_slab):
        tot = tot + y_ref[sl]
    mu = jnp.sum(tot, axis=-1, keepdims=True) * (1.0 / n_ch)
    sq = None
    for sl in range(n_slab):
        yc = y_ref[sl] - mu
        sq = yc * yc if sq is None else sq + yc * yc
    rstd = lax.rsqrt(jnp.sum(sq, axis=-1, keepdims=True) * (1.0 / n_ch) + EPS)
    out = []
    for sl in range(n_slab):
        lanes = slice(sl * LANES, (sl + 1) * LANES)
        z = (y_ref[sl] - mu) * rstd * g_ref[:, lanes] + b_ref[:, lanes]
        out.append(z * (1.0 / (1.0 + jnp.exp(-z))))
    return out


def _conv_sample_kernel(ext_ref, w_ref, cb_ref, g_ref, b_ref, o_ref):
    nb, rows, _ = o_ref.shape
    for e in range(nb):
        acc = jnp.zeros(o_ref.shape[1:], F32)
        for j in range(CONV_K):
            acc = acc + ext_ref[e, j:j + rows, :] * w_ref[j:j + 1, :]
        y = acc + cb_ref[...]
        o_ref[e] = _ln_swish(y, g_ref[...], b_ref[...]).astype(BF16)


CONV_SAMPLE_ROWS = 8
CONV_SAMPLE_ENTRIES = 8


def _conv_sample(ext, conv_w, conv_b, ln_g, ln_b, dec_seq):
    nb, n_rows, c = ext.shape
    assert dec_seq <= CONV_SAMPLE_ROWS and nb % CONV_SAMPLE_ENTRIES == 0
    pad_rows = (CONV_K - 1) + CONV_SAMPLE_ROWS
    ext_p = jnp.pad(ext, ((0, 0), (0, pad_rows - n_rows), (0, 0)))
    small = lambda b: (0, 0)
    o = pl.pallas_call(
        _conv_sample_kernel,
        grid=(nb // CONV_SAMPLE_ENTRIES,),
        in_specs=[pl.BlockSpec((CONV_SAMPLE_ENTRIES, pad_rows, c), lambda b: (b, 0, 0)),
                  pl.BlockSpec((CONV_K, c), small),
                  pl.BlockSpec((1, c), small),
                  pl.BlockSpec((1, c), small),
                  pl.BlockSpec((1, c), small)],
        out_specs=pl.BlockSpec((CONV_SAMPLE_ENTRIES, CONV_SAMPLE_ROWS, c), lambda b: (b, 0, 0)),
        out_shape=jax.ShapeDtypeStruct((nb, CONV_SAMPLE_ROWS, c), BF16),
        compiler_params=_cparams(("parallel",)),
        name="conv_sample",
    )(ext_p, conv_w, conv_b, ln_g, ln_b)
    return o[:, :dec_seq].reshape(nb * dec_seq, c)


def _outproj_kernel(o_ref, c_ref, x_ref, wa_ref, wc_ref, g_ref, h_ref, hn_ref):
    h = x_ref[...] + jnp.dot(o_ref[...], wa_ref[...], preferred_element_type=F32) \
        + jnp.dot(c_ref[...], wc_ref[...], preferred_element_type=F32)
    h_ref[...] = h
    ms = jnp.mean(h * h, axis=-1, keepdims=True)
    hn_ref[...] = (h * lax.rsqrt(ms + EPS) * g_ref[...]).astype(BF16)


def _outproj(o_n, c, x, w_out_bf, g2, tm):
    m, d = x.shape
    wa = o_n.shape[1]
    row = lambda i: (i, 0)
    return pl.pallas_call(
        _outproj_kernel,
        grid=(m // tm,),
        in_specs=[pl.BlockSpec((tm, wa), row),
                  pl.BlockSpec((tm, c.shape[1]), row),
                  pl.BlockSpec((tm, d), row),
                  pl.BlockSpec((wa, d), lambda i: (0, 0)),
                  pl.BlockSpec((c.shape[1], d), lambda i: (wa // c.shape[1], 0)),
                  pl.BlockSpec((1, d), lambda i: (0, 0))],
        out_specs=[pl.BlockSpec((tm, d), row), pl.BlockSpec((tm, d), row)],
        out_shape=[jax.ShapeDtypeStruct((m, d), F32), jax.ShapeDtypeStruct((m, d), BF16)],
        compiler_params=_cparams(("parallel",)),
        name="outproj",
    )(o_n, c, x, w_out_bf, w_out_bf, g2)


def _outproj_conv_kernel(o_ref, cur_ref, tail_ref, x_ref, wa_ref, wc_ref, cw_ref, cb_ref, lg_ref,
                         lb_ref, g_ref, h_ref, hn_ref, ext_ref, y_ref, *, tt):
    slabs = _conv_block(pl.program_id(1), cur_ref, tail_ref, cw_ref, cb_ref, lg_ref, lb_ref,
                        ext_ref, y_ref, tt)
    c = jnp.concatenate([s.astype(BF16) for s in slabs], axis=1)
    h = x_ref[...] + jnp.dot(o_ref[...], wa_ref[...], preferred_element_type=F32) \
        + jnp.dot(c, wc_ref[...], preferred_element_type=F32)
    h_ref[...] = h
    ms = jnp.mean(h * h, axis=-1, keepdims=True)
    hn_ref[...] = (h * lax.rsqrt(ms + EPS) * g_ref[...]).astype(BF16)


def _outproj_conv(o_n, u, x, w_out_bf, conv_w, conv_b, ln_g, ln_b, g2, batch, seq, tt, o_block):
    m, d = x.shape
    wa = o_n.shape[1]
    c = u.shape[1]
    nt = seq // tt
    ratio = tt // CONV_HALO
    row = lambda b, t: (b * nt + t, 0)
    tail = lambda b, t: (jnp.maximum((b * nt + t) * ratio - 1, 0), 0)
    small = lambda b, t: (0, 0)
    return pl.pallas_call(
        functools.partial(_outproj_conv_kernel, tt=tt),
        grid=(batch, nt),
        in_specs=[pl.BlockSpec((tt, wa), lambda b, t: (o_block(b * nt + t), 0)),
                  pl.BlockSpec((tt, c), row),
                  pl.BlockSpec((CONV_HALO, c), tail),
                  pl.BlockSpec((tt, d), row),
                  pl.BlockSpec((wa, d), small),
                  pl.BlockSpec((c, d), lambda b, t: (wa // c, 0)),
                  pl.BlockSpec((CONV_K, c), small),
                  pl.BlockSpec((1, c), small),
                  pl.BlockSpec((1, c), small),
                  pl.BlockSpec((1, c), small),
                  pl.BlockSpec((1, d), small)],
        out_specs=[pl.BlockSpec((tt, d), row), pl.BlockSpec((tt, d), row)],
        out_shape=[jax.ShapeDtypeStruct((m, d), F32), jax.ShapeDtypeStruct((m, d), BF16)],
        scratch_shapes=[pltpu.VMEM((c // LANES, CONV_HALO + tt, LANES), F32),
                        pltpu.VMEM((c // LANES, tt, LANES), F32)],
        compiler_params=_cparams(("parallel", "parallel")),
        name="outproj_conv",
    )(o_n, u, u, x, w_out_bf, w_out_bf, conv_w, conv_b, ln_g, ln_b, g2)


def _ffn_kernel(hn_ref, h_ref, wu_ref, wd_ref, g_ref, y_ref, acc_ref):
    f = pl.program_id(1)

    @pl.when(f == 0)
    def _():
        acc_ref[...] = jnp.zeros(acc_ref.shape, F32)

    a = jnp.maximum(jnp.dot(hn_ref[...], wu_ref[...], preferred_element_type=F32), 0.0)
    acc_ref[...] += jnp.dot((a * a).astype(BF16), wd_ref[...], preferred_element_type=F32)

    @pl.when(f == pl.num_programs(1) - 1)
    def _():
        y = h_ref[...] + acc_ref[...]
        ms = jnp.mean(y * y, axis=-1, keepdims=True)
        y_ref[...] = y * lax.rsqrt(ms + EPS) * g_ref[...]


def _ffn(hn, h, w_up_bf, w_down_bf, final_g, tm, tf):
    m, d = h.shape
    dff = w_up_bf.shape[1]
    row = lambda i, f: (i, 0)
    return pl.pallas_call(
        _ffn_kernel,
        grid=(m // tm, dff // tf),
        in_specs=[pl.BlockSpec((tm, d), row),
                  pl.BlockSpec((tm, d), row),
                  pl.BlockSpec((d, tf), lambda i, f: (0, f)),
                  pl.BlockSpec((tf, d), lambda i, f: (f, 0)),
                  pl.BlockSpec((1, d), lambda i, f: (0, 0))],
        out_specs=pl.BlockSpec((tm, d), row),
        out_shape=jax.ShapeDtypeStruct((m, d), F32),
        scratch_shapes=[pltpu.VMEM((tm, d), F32)],
        compiler_params=_cparams(("parallel", "arbitrary")),
        name="ffn",
    )(hn, h, w_up_bf, w_down_bf, final_g)


def kernel(x_prompt, x_sample, cache_k, cache_v, state_conv, page_table, norm1_g, w_in,
           lambda_q1, lambda_k1, lambda_q2, lambda_k2, subln_g, conv_w, conv_b,
           conv_ln_g, conv_ln_b, w_out, norm2_g, w_up, w_down, final_g):
    depth = w_in.shape[0]
    assert depth == 1, "final norm is fused into the (single) layer's FFN kernel"
    batch, seq, d = x_prompt.shape
    dec_batch, dec_seq, _ = x_sample.shape
    n_heads, head_w = cache_k.shape[3], cache_k.shape[4]
    assert head_w == HEAD_W
    past = page_table.shape[1] * cache_k.shape[2]
    conv_c = conv_w.shape[2]

    xp = x_prompt.reshape(batch * seq, d)
    xs = x_sample.reshape(dec_batch * dec_seq, d)
    tm_p = 512
    tm_s = dec_batch * dec_seq

    tabs_p = _rope_tables(np.arange(seq))
    tabs_s = _rope_tables(np.tile(past + np.arange(dec_seq), dec_batch))

    l = 0
    lam_init = 0.8 - 0.6 * math.exp(-0.3 * l)
    row = lambda a: a[l].reshape(1, -1)
    lams = (row(lambda_q1), row(lambda_k1), row(lambda_q2), row(lambda_k2))
    g1, g2, sg = row(norm1_g), row(norm2_g), row(subln_g)
    cb, lg, lb = row(conv_b), row(conv_ln_g), row(conv_ln_b)
    fg = final_g.reshape(1, -1)
    w_in_bf = w_in[l].astype(BF16)
    w_out_bf = w_out[l].astype(BF16)
    w_up_bf = w_up[l].astype(BF16)
    w_down_bf = w_down[l].astype(BF16)
    cw = conv_w[l]

    q_p, k_p, kb_p, v_p, vb_p, u_p = _inproj(xp, g1, w_in_bf, tabs_p, tm_p, seq // tm_p)
    q_s, k_s, kb_s, v_s, vb_s, u_s = _inproj(xs, g1, w_in_bf, tabs_s, tm_s, 1)
    tq = 256
    o_p, o_s = _attention(q_p, kb_p, vb_p, q_s, kb_s, vb_s, cache_k[l], cache_v[l], page_table,
                          lams, sg, batch, seq, dec_batch, dec_seq, lam_init, tq)

    h_p, hn_p = _outproj_conv(o_p, u_p, xp, w_out_bf, cw, cb, lg, lb, g2, batch, seq, tq,
                              functools.partial(_paired_block, nq=seq // tq))
    y_p = _ffn(hn_p, h_p, w_up_bf, w_down_bf, fg, tm_p, 1024)

    ext_s = jnp.concatenate([state_conv[l], u_s.reshape(dec_batch, dec_seq, conv_c)], axis=1)
    c_s = _conv_sample(ext_s, cw, cb, lg, lb, dec_seq)
    h_s, hn_s = _outproj(o_s, c_s, xs, w_out_bf, g2, tm_s)
    y_s = _ffn(hn_s, h_s, w_up_bf, w_down_bf, fg, tm_s, 1024)

    keep = CONV_K - 1
    return (y_p.reshape(batch, seq, d),
            y_s.reshape(dec_batch, dec_seq, d),
            k_p.reshape(1, batch, seq, n_heads, head_w),
            v_p.reshape(1, batch, seq, n_heads, head_w),
            u_p.reshape(batch, seq, conv_c)[:, seq - keep:][None],
            k_s.reshape(1, dec_batch, dec_seq, n_heads, head_w),
            v_s.reshape(1, dec_batch, dec_seq, n_heads, head_w),
            ext_s[:, dec_seq:][None])
```

```python
import functools
import math

import numpy as np
import jax
import jax.numpy as jnp
from jax import lax
from jax.experimental import pallas as pl
from jax.experimental.pallas import tpu as pltpu

F32 = jnp.float32
BF16 = jnp.bfloat16

EPS = 1e-6
ROPE_THETA = 500000.0
LANES = 128
HEAD_W = 128
QK_DIM = 64
ROPE_DIM = 16
CONV_K = 31
NEG = -1e30
VMEM_LIMIT = 56 * 1024 * 1024


def _cparams(sem):
    return pltpu.CompilerParams(dimension_semantics=sem, vmem_limit_bytes=VMEM_LIMIT)


def _rope(z, c, s1, s2):
    return z * c + pltpu.roll(z, LANES - 8, 1) * s1 + pltpu.roll(z, 8, 1) * s2


def _inproj_kernel(x_ref, g_ref, w_ref, c_ref, s1_ref, s2_ref, *refs, emit_w):
    if emit_w:
        wb_ref, refs = refs[0], refs[1:]
    q_ref, k_ref, kb_ref, v_ref, vb_ref, u_ref, xn_ref, za_ref, zb_ref = refs
    j = pl.program_id(1)
    n_heads = za_ref.shape[1] // HEAD_W

    def matmul():
        w = w_ref[...]
        if emit_w:
            w = w.astype(BF16)
            wb_ref[...] = w
        return jnp.dot(xn_ref[...], w, preferred_element_type=F32)

    def roped(z_ref):
        c, s1, s2 = c_ref[...], s1_ref[...], s2_ref[...]
        for h in range(n_heads):
            sl = slice(h * HEAD_W, (h + 1) * HEAD_W)
            yield sl, _rope(z_ref[:, sl], c, s1, s2)

    @pl.when(j == 0)
    def _():
        x = x_ref[...]
        ms = jnp.mean(x * x, axis=-1, keepdims=True)
        xn_ref[...] = (x * lax.rsqrt(ms + EPS) * g_ref[...]).astype(BF16)
        za_ref[...] = matmul()

    @pl.when(j == 1)
    def _():
        zb_ref[...] = matmul()
        for sl, qr in roped(za_ref):
            q_ref[:, sl] = (qr * (QK_DIM ** -0.5)).astype(BF16)

    @pl.when(j == 2)
    def _():
        za_ref[...] = matmul()
        for sl, kr in roped(zb_ref):
            k_ref[:, sl] = kr
            kb_ref[:, sl] = kr.astype(BF16)

    @pl.when(j == 3)
    def _():
        zb_ref[...] = matmul()
        v = za_ref[...]
        v_ref[...] = v
        vb_ref[...] = v.astype(BF16)

    @pl.when(j == 4)
    def _():
        u_ref[...] = zb_ref[...] * (1.0 / (1.0 + jnp.exp(-matmul())))


def _inproj(x, g, w, tabs, tm, n_pos_blocks):
    m, d = x.shape
    nw = w.shape[1] // 5
    n_rb = m // tm
    emit_w = w.dtype != BF16
    row = lambda i, j: (i, 0)
    tab = lambda i, j: (i % n_pos_blocks, 0)
    w_col = lambda i, j: (0, j)

    def written_at(step):
        return lambda i, j: (jnp.minimum(i + (j > step).astype(jnp.int32), n_rb - 1), 0)

    out_f = jax.ShapeDtypeStruct((m, nw), F32)
    out_b = jax.ShapeDtypeStruct((m, nw), BF16)
    out_specs = [pl.BlockSpec((tm, nw), written_at(1)),
                 pl.BlockSpec((tm, nw), written_at(2)), pl.BlockSpec((tm, nw), written_at(2)),
                 pl.BlockSpec((tm, nw), written_at(3)), pl.BlockSpec((tm, nw), written_at(3)),
                 pl.BlockSpec((tm, nw), row)]
    out_shape = [out_b, out_f, out_b, out_f, out_b, out_f]
    if emit_w:
        out_specs = [pl.BlockSpec((d, nw), w_col)] + out_specs
        out_shape = [jax.ShapeDtypeStruct(w.shape, BF16)] + out_shape
    return pl.pallas_call(
        functools.partial(_inproj_kernel, emit_w=emit_w),
        grid=(n_rb, 5),
        in_specs=[pl.BlockSpec((tm, d), row),
                  pl.BlockSpec((1, d), lambda i, j: (0, 0)),
                  pl.BlockSpec((d, nw), w_col),
                  pl.BlockSpec((tm, LANES), tab),
                  pl.BlockSpec((tm, LANES), tab),
                  pl.BlockSpec((tm, LANES), tab)],
        out_specs=out_specs,
        out_shape=out_shape,
        scratch_shapes=[pltpu.VMEM((tm, d), BF16), pltpu.VMEM((tm, nw), F32),
                        pltpu.VMEM((tm, nw), F32)],
        compiler_params=_cparams(("arbitrary", "arbitrary")),
        name="inproj",
    )(x, g, w, *tabs)


def _rope_tables(pos):
    pos = np.asarray(pos, np.float64)
    half = ROPE_DIM // 2
    inv = np.power(ROPE_THETA, -np.arange(half, dtype=np.float64) * 2.0 / ROPE_DIM)
    ang = pos[:, None] * inv[None, :]
    cos, sin = np.cos(ang), np.sin(ang)
    t = pos.shape[0]
    one = np.ones((t, QK_DIM - ROPE_DIM))
    zero = np.zeros((t, QK_DIM - ROPE_DIM))
    zh = np.zeros((t, half))
    c = np.concatenate([cos, cos, one], axis=-1)
    s1 = np.concatenate([-sin, zh, zero], axis=-1)
    s2 = np.concatenate([zh, sin, zero], axis=-1)
    dup = lambda a: jnp.asarray(np.concatenate([a, a], axis=-1), F32)
    return dup(c), dup(s1), dup(s2)


def _lam(lq1, lk1, lq2, lk2, lam_init):
    a = jnp.sum(lq1[...] * lk1[...], axis=-1, keepdims=True)
    b = jnp.sum(lq2[...] * lk2[...], axis=-1, keepdims=True)
    return jnp.exp(a) - jnp.exp(b) + lam_init


def _subln(o, g, lam_init):
    ms = jnp.mean(o * o, axis=-1, keepdims=True)
    return (o * lax.rsqrt(ms + EPS) * g) * (1.0 - lam_init)


def _fold_lanes(x, op):
    f = x[:, 0:LANES]
    for t in range(1, x.shape[1] // LANES):
        f = op(f, x[:, t * LANES:(t + 1) * LANES])
    return f


ATTN_SEG = 3


def _attn_kernel(pt_ref, lq1, lk1, lq2, lk2, sg_ref, qlo_ref, qhi_ref, k_ref, v_ref,
                 qs_ref, bias_ref, nbias_ref, kn_ref, vn_ref, *rest, tq, nq, n_pp, lam_init):
    del pt_ref
    pk_refs = rest[:n_pp]
    pv_refs = rest[n_pp:2 * n_pp]
    o_ref, os_ref = rest[2 * n_pp:2 * n_pp + 2]
    qq_lo, qq_hi, s_lo, s_hi, m_ref, l_ref, acc_ref = rest[2 * n_pp + 2:]
    i = pl.program_id(2)
    lane = lax.broadcasted_iota(jnp.int32, (tq, HEAD_W), 1)
    for q_ref, qq_ref in ((qlo_ref, qq_lo), (qhi_ref, qq_hi)):
        q = q_ref[...].astype(F32)
        qq_ref[0:tq, :] = jnp.where(lane < QK_DIM, q, 0.0).astype(BF16)
        qq_ref[tq:2 * tq, :] = jnp.where(lane >= QK_DIM, q, 0.0).astype(BF16)

    def segment(qq_ref, s_ref, blocks, diag):
        mx = None
        for c in blocks:
            cols = slice(c * tq, (c + 1) * tq)
            s = lax.dot_general(qq_ref[...], k_ref[cols, :], (((1,), (1,)), ((), ())),
                                preferred_element_type=F32)
            if c == diag:
                r = lax.broadcasted_iota(jnp.int32, s.shape, 0)
                col = lax.broadcasted_iota(jnp.int32, s.shape, 1)
                r = jnp.where(r >= tq, r - tq, r)
                s = jnp.where(col <= r, s, NEG)
            s_ref[:, cols] = s
            f = _fold_lanes(s, jnp.maximum)
            mx = f if mx is None else jnp.maximum(mx, f)
        m = jnp.max(mx, axis=-1, keepdims=True)
        ls = None
        pv = None
        for c in blocks:
            cols = slice(c * tq, (c + 1) * tq)
            p = jnp.exp(s_ref[:, cols] - m)
            f = _fold_lanes(p, jnp.add)
            ls = f if ls is None else ls + f
            d = jnp.dot(p.astype(BF16), v_ref[cols, :], preferred_element_type=F32)
            pv = d if pv is None else pv + d
        return m, ls, pv

    def attend(qq_ref, s_ref, n_blk):
        parts = [segment(qq_ref, s_ref, range(c0, min(c0 + ATTN_SEG, n_blk)), n_blk - 1)
                 for c0 in range(0, n_blk, ATTN_SEG)]
        m = parts[0][0]
        for mp, _, _ in parts[1:]:
            m = jnp.maximum(m, mp)
        ls = None
        pv = None
        for mp, lsp, pvp in parts:
            w = jnp.exp(mp - m)
            ls = w * lsp if ls is None else ls + w * lsp
            pv = w * pvp if pv is None else pv + w * pvp
        accn = pv / jnp.sum(ls, axis=-1, keepdims=True)
        lam = _lam(lq1, lk1, lq2, lk2, lam_init)
        o = accn[0:tq, :] - lam * accn[tq:2 * tq, :]
        return _subln(o, sg_ref[...], lam_init).astype(BF16)

    def paged_update(kbs, vbs, bias):
        qs = qs_ref[...]
        ss = [lax.dot_general(qs, kb, (((1,), (1,)), ((), ())), preferred_element_type=F32) + bias
              for kb in kbs]
        mx = _fold_lanes(ss[0], jnp.maximum)
        for s in ss[1:]:
            mx = jnp.maximum(mx, _fold_lanes(s, jnp.maximum))
        m_prev = m_ref[...]
        m_new = jnp.maximum(m_prev, jnp.max(mx, axis=-1, keepdims=True))
        alpha = jnp.exp(m_prev - m_new)
        ls = None
        pv = None
        for s, vb in zip(ss, vbs):
            p = jnp.exp(s - m_new)
            f = _fold_lanes(p, jnp.add)
            ls = f if ls is None else ls + f
            d = jnp.dot(p.astype(BF16), vb, preferred_element_type=F32)
            pv = d if pv is None else pv + d
        l_ref[...] = alpha * l_ref[...] + jnp.sum(ls, axis=-1, keepdims=True)
        acc_ref[...] = alpha * acc_ref[...] + pv
        m_ref[...] = m_new

    def paged(step):
        if step == 0:
            m_ref[...] = jnp.full(m_ref.shape, NEG, F32)
            l_ref[...] = jnp.zeros(l_ref.shape, F32)
            acc_ref[...] = jnp.zeros(acc_ref.shape, F32)
            paged_update([kn_ref[...]], [vn_ref[...]], nbias_ref[...])
        paged_update([r[...].astype(BF16) for r in pk_refs],
                     [r[...].astype(BF16) for r in pv_refs], bias_ref[...])
        if step == nq // 2 - 1:
            accn = acc_ref[...] / l_ref[...]
            half = accn.shape[0] // 2
            lam = _lam(lq1, lk1, lq2, lk2, lam_init)
            o = accn[0:half, :] - lam * accn[half:, :]
            os_ref[...] = _subln(o, sg_ref[...], lam_init).astype(BF16)

    def variant(step):
        paged(step)
        o_ref[0:tq, :] = attend(qq_lo, s_lo, step + 1)
        o_ref[tq:2 * tq, :] = attend(qq_hi, s_hi, nq - step)

    for n in range(nq // 2):
        pl.when(i == n)(functools.partial(variant, n))


def _paired_block(r, nq):
    b, rb = r // nq, r % nq
    return b * nq + jnp.where(rb < nq // 2, 2 * rb, 2 * (nq - 1 - rb) + 1)


def _attention(q_bf, k_bf, v_bf, q_s, kn_bf, vn_bf, cache_k, cache_v, page_table, lams, subln_g,
               batch, seq, dec_batch, dec_seq, lam_init, tq):
    m, w = q_bf.shape
    n_heads = w // HEAD_W
    nq = seq // tq
    half = nq // 2
    n_pool, page, _, _ = cache_k.shape
    n_pages = page_table.shape[1]
    n_pp = n_pages // half
    assert nq % 2 == 0 and n_pp * half == n_pages and batch * n_heads == dec_batch
    rows_q = dec_seq * n_heads
    page_rows = page * n_heads
    q4 = q_s.reshape(dec_batch, dec_seq, n_heads, HEAD_W).transpose(0, 2, 1, 3)
    q4 = q4.reshape(dec_batch, rows_q, HEAD_W)
    lane = jnp.arange(HEAD_W)
    qt = jnp.concatenate([jnp.where(lane < QK_DIM, q4, 0), jnp.where(lane >= QK_DIM, q4, 0)],
                         axis=1).astype(BF16)
    pad = lambda a: jnp.pad(a.reshape(dec_batch, rows_q, HEAD_W),
                            ((0, 0), (0, LANES - rows_q), (0, 0)))
    kn, vn = pad(kn_bf), pad(vn_bf)
    c = np.arange(2 * rows_q)
    c_head = (c % rows_q) // dec_seq
    c_query = c % dec_seq
    r = np.arange(page_rows)
    bias = jnp.asarray(np.where((r % n_heads)[None, :] == c_head[:, None], 0.0, NEG), F32)
    rn = np.arange(LANES)
    ok = ((rn % n_heads)[None, :] == c_head[:, None]) & ((rn // n_heads)[None, :] <= c_query[:, None]) \
        & (rn < rows_q)[None, :]
    nbias = jnp.asarray(np.where(ok, 0.0, NEG), F32)
    ck = cache_k.reshape(n_pool, page_rows, HEAD_W)
    cv = cache_v.reshape(n_pool, page_rows, HEAD_W)

    small = lambda b, h, i, pt: (0, 0)
    kv = lambda b, h, i, pt: (b, h)
    per_entry = lambda b, h, i, pt: (b * n_heads + h, 0, 0)

    def page_spec(t):
        return pl.BlockSpec((None, page_rows, HEAD_W),
                            lambda b, h, i, pt: (pt[b * n_heads + h, i * n_pp + t], 0, 0))

    grid_spec = pltpu.PrefetchScalarGridSpec(
        num_scalar_prefetch=1,
        grid=(batch, n_heads, half),
        in_specs=[pl.BlockSpec((1, QK_DIM), small)] * 4
                 + [pl.BlockSpec((1, HEAD_W), small),
                    pl.BlockSpec((tq, HEAD_W), lambda b, h, i, pt: (b * nq + i, h)),
                    pl.BlockSpec((tq, HEAD_W), lambda b, h, i, pt: (b * nq + nq - 1 - i, h)),
                    pl.BlockSpec((seq, HEAD_W), kv),
                    pl.BlockSpec((seq, HEAD_W), kv),
                    pl.BlockSpec((None, 2 * rows_q, HEAD_W), per_entry),
                    pl.BlockSpec((2 * rows_q, page_rows), small),
                    pl.BlockSpec((2 * rows_q, LANES), small),
                    pl.BlockSpec((None, LANES, HEAD_W), per_entry),
                    pl.BlockSpec((None, LANES, HEAD_W), per_entry)]
                 + [page_spec(t) for t in range(n_pp)] * 2,
        out_specs=[pl.BlockSpec((2 * tq, HEAD_W), lambda b, h, i, pt: (b * half + i, h)),
                   pl.BlockSpec((None, rows_q, HEAD_W), per_entry)],
        scratch_shapes=[pltpu.VMEM((2 * tq, HEAD_W), BF16),
                        pltpu.VMEM((2 * tq, HEAD_W), BF16),
                        pltpu.VMEM((2 * tq, half * tq), F32),
                        pltpu.VMEM((2 * tq, seq), F32),
                        pltpu.VMEM((2 * rows_q, 1), F32),
                        pltpu.VMEM((2 * rows_q, 1), F32),
                        pltpu.VMEM((2 * rows_q, HEAD_W), F32)],
    )
    o_p, o_s = pl.pallas_call(
        functools.partial(_attn_kernel, tq=tq, nq=nq, n_pp=n_pp, lam_init=lam_init),
        grid_spec=grid_spec,
        out_shape=[jax.ShapeDtypeStruct((m, w), BF16),
                   jax.ShapeDtypeStruct((dec_batch, rows_q, HEAD_W), BF16)],
        compiler_params=_cparams(("arbitrary", "arbitrary", "arbitrary")),
        name="attention",
    )(page_table, *lams, subln_g, q_bf, q_bf, k_bf, v_bf, qt, bias, nbias, kn, vn,
      *([ck] * n_pp), *([cv] * n_pp))
    o_s = o_s.reshape(dec_batch, n_heads, dec_seq, HEAD_W).transpose(0, 2, 1, 3)
    return o_p, o_s.reshape(dec_batch * dec_seq, n_heads * HEAD_W)


def _ln_swish(y, g, b):
    mu = jnp.mean(y, axis=-1, keepdims=True)
    yc = y - mu
    var = jnp.mean(yc * yc, axis=-1, keepdims=True)
    z = yc * lax.rsqrt(var + EPS) * g + b
    return z * (1.0 / (1.0 + jnp.exp(-z)))


CONV_HALO = 32
CONV_CHUNK = 16


def _conv_block(t, cur_ref, tail_ref, w_ref, cb_ref, g_ref, b_ref, ext_ref, y_ref, tt):
    n_slab = cur_ref.shape[1] // LANES
    for sl in range(n_slab):
        lanes = slice(sl * LANES, (sl + 1) * LANES)
        tail = tail_ref[:, lanes]
        ext_ref[sl, 0:CONV_HALO, :] = jnp.where(t == 0, jnp.zeros_like(tail), tail)
        ext_ref[sl, CONV_HALO:CONV_HALO + tt, :] = cur_ref[:, lanes]
    off = CONV_HALO - (CONV_K - 1)
    n_ch = cur_ref.shape[1]

    half = CONV_CHUNK // 2
    for base in range(0, tt, CONV_CHUNK):
        for sl in range(n_slab):
            lanes = slice(sl * LANES, (sl + 1) * LANES)
            xs = [ext_ref[sl, pl.ds(base + off + k, half, stride=2), :] for k in range(CONV_K + 1)]
            even = jnp.zeros((half, LANES), F32)
            odd = jnp.zeros((half, LANES), F32)
            for j in range(CONV_K):
                w = w_ref[j:j + 1, lanes]
                even = even + xs[j] * w
                odd = odd + xs[j + 1] * w
            y_ref[sl, pl.ds(base, half, stride=2), :] = even + cb_ref[:, lanes]
            y_ref[sl, pl.ds(base + 1, half, stride=2), :] = odd + cb_ref[:, lanes]

    tot = y_ref[0]
    for sl in range(1, n_slab):
        tot = tot + y_ref[sl]
    mu = jnp.sum(tot, axis=-1, keepdims=True) * (1.0 / n_ch)
    sq = None
    for sl in range(n_slab):
        yc = y_ref[sl] - mu
        sq = yc * yc if sq is None else sq + yc * yc
    rstd = lax.rsqrt(jnp.sum(sq, axis=-1, keepdims=True) * (1.0 / n_ch) + EPS)
    out = []
    for sl in range(n_slab):
        lanes = slice(sl * LANES, (sl + 1) * LANES)
        z = (y_ref[sl] - mu) * rstd * g_ref[:, lanes] + b_ref[:, lanes]
        out.append(z * (1.0 / (1.0 + jnp.exp(-z))))
    return out


def _conv_sample_kernel(ext_ref, w_ref, cb_ref, g_ref, b_ref, o_ref):
    nb, rows, _ = o_ref.shape
    for e in range(nb):
        acc = jnp.zeros(o_ref.shape[1:], F32)
        for j in range(CONV_K):
            acc = acc + ext_ref[e, j:j + rows, :] * w_ref[j:j + 1, :]
        y = acc + cb_ref[...]
        o_ref[e] = _ln_swish(y, g_ref[...], b_ref[...]).astype(BF16)


CONV_SAMPLE_ROWS = 8
CONV_SAMPLE_ENTRIES = 8


def _conv_sample(ext, conv_w, conv_b, ln_g, ln_b, dec_seq):
    nb, n_rows, c = ext.shape
    assert dec_seq <= CONV_SAMPLE_ROWS and nb % CONV_SAMPLE_ENTRIES == 0
    pad_rows = (CONV_K - 1) + CONV_SAMPLE_ROWS
    ext_p = jnp.pad(ext, ((0, 0), (0, pad_rows - n_rows), (0, 0)))
    small = lambda b: (0, 0)
    o = pl.pallas_call(
        _conv_sample_kernel,
        grid=(nb // CONV_SAMPLE_ENTRIES,),
        in_specs=[pl.BlockSpec((CONV_SAMPLE_ENTRIES, pad_rows, c), lambda b: (b, 0, 0)),
                  pl.BlockSpec((CONV_K, c), small),
                  pl.BlockSpec((1, c), small),
                  pl.BlockSpec((1, c), small),
                  pl.BlockSpec((1, c), small)],
        out_specs=pl.BlockSpec((CONV_SAMPLE_ENTRIES, CONV_SAMPLE_ROWS, c), lambda b: (b, 0, 0)),
        out_shape=jax.ShapeDtypeStruct((nb, CONV_SAMPLE_ROWS, c), BF16),
        compiler_params=_cparams(("parallel",)),
        name="conv_sample",
    )(ext_p, conv_w, conv_b, ln_g, ln_b)
    return o[:, :dec_seq].reshape(nb * dec_seq, c)


def _outproj_kernel(o_ref, c_ref, x_ref, wa_ref, wc_ref, g_ref, h_ref, hn_ref):
    h = x_ref[...] + jnp.dot(o_ref[...], wa_ref[...], preferred_element_type=F32) \
        + jnp.dot(c_ref[...], wc_ref[...], preferred_element_type=F32)
    h_ref[...] = h
    ms = jnp.mean(h * h, axis=-1, keepdims=True)
    hn_ref[...] = (h * lax.rsqrt(ms + EPS) * g_ref[...]).astype(BF16)


def _outproj(o_n, c, x, w_out_bf, g2, tm):
    m, d = x.shape
    wa = o_n.shape[1]
    row = lambda i: (i, 0)
    return pl.pallas_call(
        _outproj_kernel,
        grid=(m // tm,),
        in_specs=[pl.BlockSpec((tm, wa), row),
                  pl.BlockSpec((tm, c.shape[1]), row),
                  pl.BlockSpec((tm, d), row),
                  pl.BlockSpec((wa, d), lambda i: (0, 0)),
                  pl.BlockSpec((c.shape[1], d), lambda i: (wa // c.shape[1], 0)),
                  pl.BlockSpec((1, d), lambda i: (0, 0))],
        out_specs=[pl.BlockSpec((tm, d), row), pl.BlockSpec((tm, d), row)],
        out_shape=[jax.ShapeDtypeStruct((m, d), F32), jax.ShapeDtypeStruct((m, d), BF16)],
        compiler_params=_cparams(("parallel",)),
        name="outproj",
    )(o_n, c, x, w_out_bf, w_out_bf, g2)


def _outproj_conv_kernel(o_ref, cur_ref, tail_ref, x_ref, wa_ref, wc_ref, cw_ref, cb_ref, lg_ref,
                         lb_ref, g_ref, h_ref, hn_ref, ext_ref, y_ref, *, tt):
    slabs = _conv_block(pl.program_id(1), cur_ref, tail_ref, cw_ref, cb_ref, lg_ref, lb_ref,
                        ext_ref, y_ref, tt)
    c = jnp.concatenate([s.astype(BF16) for s in slabs], axis=1)
    h = x_ref[...] + jnp.dot(o_ref[...], wa_ref[...], preferred_element_type=F32) \
        + jnp.dot(c, wc_ref[...], preferred_element_type=F32)
    h_ref[...] = h
    ms = jnp.mean(h * h, axis=-1, keepdims=True)
    hn_ref[...] = (h * lax.rsqrt(ms + EPS) * g_ref[...]).astype(BF16)


def _outproj_conv(o_n, u, x, w_out_bf, conv_w, conv_b, ln_g, ln_b, g2, batch, seq, tt, o_block):
    m, d = x.shape
    wa = o_n.shape[1]
    c = u.shape[1]
    nt = seq // tt
    ratio = tt // CONV_HALO
    row = lambda b, t: (b * nt + t, 0)
    tail = lambda b, t: (jnp.maximum((b * nt + t) * ratio - 1, 0), 0)
    small = lambda b, t: (0, 0)
    return pl.pallas_call(
        functools.partial(_outproj_conv_kernel, tt=tt),
        grid=(batch, nt),
        in_specs=[pl.BlockSpec((tt, wa), lambda b, t: (o_block(b * nt + t), 0)),
                  pl.BlockSpec((tt, c), row),
                  pl.BlockSpec((CONV_HALO, c), tail),
                  pl.BlockSpec((tt, d), row),
                  pl.BlockSpec((wa, d), small),
                  pl.BlockSpec((c, d), lambda b, t: (wa // c, 0)),
                  pl.BlockSpec((CONV_K, c), small),
                  pl.BlockSpec((1, c), small),
                  pl.BlockSpec((1, c), small),
                  pl.BlockSpec((1, c), small),
                  pl.BlockSpec((1, d), small)],
        out_specs=[pl.BlockSpec((tt, d), row), pl.BlockSpec((tt, d), row)],
        out_shape=[jax.ShapeDtypeStruct((m, d), F32), jax.ShapeDtypeStruct((m, d), BF16)],
        scratch_shapes=[pltpu.VMEM((c // LANES, CONV_HALO + tt, LANES), F32),
                        pltpu.VMEM((c // LANES, tt, LANES), F32)],
        compiler_params=_cparams(("parallel", "parallel")),
        name="outproj_conv",
    )(o_n, u, u, x, w_out_bf, w_out_bf, conv_w, conv_b, ln_g, ln_b, g2)


def _ffn_kernel(hn_ref, h_ref, wu_ref, wd_ref, g_ref, *refs, emit_w):
    if emit_w:
        wub_ref, wdb_ref, refs = refs[0], refs[1], refs[2:]
    y_ref, acc_ref = refs
    f = pl.program_id(1)

    @pl.when(f == 0)
    def _():
        acc_ref[...] = jnp.zeros(acc_ref.shape, F32)

    wu, wd = wu_ref[...], wd_ref[...]
    if emit_w:
        wu, wd = wu.astype(BF16), wd.astype(BF16)
        wub_ref[...] = wu
        wdb_ref[...] = wd
    a = jnp.maximum(jnp.dot(hn_ref[...], wu, preferred_element_type=F32), 0.0)
    acc_ref[...] += jnp.dot((a * a).astype(BF16), wd, preferred_element_type=F32)

    @pl.when(f == pl.num_programs(1) - 1)
    def _():
        y = h_ref[...] + acc_ref[...]
        ms = jnp.mean(y * y, axis=-1, keepdims=True)
        y_ref[...] = y * lax.rsqrt(ms + EPS) * g_ref[...]


def _ffn(hn, h, w_up, w_down, final_g, tm, tf):
    m, d = h.shape
    dff = w_up.shape[1]
    emit_w = w_up.dtype != BF16
    assert not emit_w or m == tm, "each weight tile must be visited exactly once to emit it"
    row = lambda i, f: (i, 0)
    up = lambda i, f: (0, f)
    down = lambda i, f: (f, 0)
    out_specs = [pl.BlockSpec((tm, d), row)]
    out_shape = [jax.ShapeDtypeStruct((m, d), F32)]
    if emit_w:
        out_specs = [pl.BlockSpec((d, tf), up), pl.BlockSpec((tf, d), down)] + out_specs
        out_shape = [jax.ShapeDtypeStruct(w_up.shape, BF16),
                     jax.ShapeDtypeStruct(w_down.shape, BF16)] + out_shape
    out = pl.pallas_call(
        functools.partial(_ffn_kernel, emit_w=emit_w),
        grid=(m // tm, dff // tf),
        in_specs=[pl.BlockSpec((tm, d), row),
                  pl.BlockSpec((tm, d), row),
                  pl.BlockSpec((d, tf), up),
                  pl.BlockSpec((tf, d), down),
                  pl.BlockSpec((1, d), lambda i, f: (0, 0))],
        out_specs=out_specs,
        out_shape=out_shape,
        scratch_shapes=[pltpu.VMEM((tm, d), F32)],
        compiler_params=_cparams(("parallel", "arbitrary")),
        name="ffn",
    )(hn, h, w_up, w_down, final_g)
    return out if emit_w else out[0]


def kernel(x_prompt, x_sample, cache_k, cache_v, state_conv, page_table, norm1_g, w_in,
           lambda_q1, lambda_k1, lambda_q2, lambda_k2, subln_g, conv_w, conv_b,
           conv_ln_g, conv_ln_b, w_out, norm2_g, w_up, w_down, final_g):
    depth = w_in.shape[0]
    assert depth == 1, "final norm is fused into the (single) layer's FFN kernel"
    batch, seq, d = x_prompt.shape
    dec_batch, dec_seq, _ = x_sample.shape
    n_heads, head_w = cache_k.shape[3], cache_k.shape[4]
    assert head_w == HEAD_W
    past = page_table.shape[1] * cache_k.shape[2]
    conv_c = conv_w.shape[2]

    xp = x_prompt.reshape(batch * seq, d)
    xs = x_sample.reshape(dec_batch * dec_seq, d)
    tm_p = 512
    tm_s = dec_batch * dec_seq

    tabs_p = _rope_tables(np.arange(seq))
    tabs_s = _rope_tables(np.tile(past + np.arange(dec_seq), dec_batch))

    l = 0
    lam_init = 0.8 - 0.6 * math.exp(-0.3 * l)
    row = lambda a: a[l].reshape(1, -1)
    lams = (row(lambda_q1), row(lambda_k1), row(lambda_q2), row(lambda_k2))
    g1, g2, sg = row(norm1_g), row(norm2_g), row(subln_g)
    cb, lg, lb = row(conv_b), row(conv_ln_g), row(conv_ln_b)
    fg = final_g.reshape(1, -1)
    w_out_bf = w_out[l].astype(BF16)
    cw = conv_w[l]

    w_in_bf, q_s, k_s, kb_s, v_s, vb_s, u_s = _inproj(xs, g1, w_in[l], tabs_s, tm_s, 1)
    q_p, k_p, kb_p, v_p, vb_p, u_p = _inproj(xp, g1, w_in_bf, tabs_p, tm_p, seq // tm_p)
    tq = 256
    o_p, o_s = _attention(q_p, kb_p, vb_p, q_s, kb_s, vb_s, cache_k[l], cache_v[l], page_table,
                          lams, sg, batch, seq, dec_batch, dec_seq, lam_init, tq)

    ext_s = jnp.concatenate([state_conv[l], u_s.reshape(dec_batch, dec_seq, conv_c)], axis=1)
    c_s = _conv_sample(ext_s, cw, cb, lg, lb, dec_seq)
    h_s, hn_s = _outproj(o_s, c_s, xs, w_out_bf, g2, tm_s)
    w_up_bf, w_down_bf, y_s = _ffn(hn_s, h_s, w_up[l], w_down[l], fg, tm_s, 512)

    h_p, hn_p = _outproj_conv(o_p, u_p, xp, w_out_bf, cw, cb, lg, lb, g2, batch, seq, tq,
                              functools.partial(_paired_block, nq=seq // tq))
    y_p = _ffn(hn_p, h_p, w_up_bf, w_down_bf, fg, tm_p, 1024)

    keep = CONV_K - 1
    return (y_p.reshape(batch, seq, d),
            y_s.reshape(dec_batch, dec_seq, d),
            k_p.reshape(1, batch, seq, n_heads, head_w),
            v_p.reshape(1, batch, seq, n_heads, head_w),
            u_p.reshape(batch, seq, conv_c)[:, seq - keep:][None],
            k_s.reshape(1, dec_batch, dec_seq, n_heads, head_w),
            v_s.reshape(1, dec_batch, dec_seq, n_heads, head_w),
            ext_s[:, dec_seq:][None])
```

```python
import functools
import math

import numpy as np
import jax
import jax.numpy as jnp
from jax import lax
from jax.experimental import pallas as pl
from jax.experimental.pallas import tpu as pltpu

F32 = jnp.float32
BF16 = jnp.bfloat16

EPS = 1e-6
ROPE_THETA = 500000.0
LANES = 128
HEAD_W = 128
QK_DIM = 64
ROPE_DIM = 16
CONV_K = 31
NEG = -1e30
VMEM_LIMIT = 56 * 1024 * 1024

TM_DENSE = 512
TQ = 256
TF_PROMPT = 1024
TF_SAMPLE = 512


def _cparams(sem):
    return pltpu.CompilerParams(dimension_semantics=sem, vmem_limit_bytes=VMEM_LIMIT)


def _rope(z, c, s1, s2):
    return z * c + pltpu.roll(z, LANES - 8, 1) * s1 + pltpu.roll(z, 8, 1) * s2


def _inproj_kernel(x_ref, g_ref, w_ref, c_ref, s1_ref, s2_ref, *refs, emit_w):
    if emit_w:
        wb_ref, refs = refs[0], refs[1:]
    q_ref, k_ref, kb_ref, v_ref, vb_ref, u_ref, xn_ref, za_ref, zb_ref = refs
    j = pl.program_id(1)
    n_heads = za_ref.shape[1] // HEAD_W

    def matmul():
        w = w_ref[...]
        if emit_w:
            w = w.astype(BF16)
            wb_ref[...] = w
        return jnp.dot(xn_ref[...], w, preferred_element_type=F32)

    def roped(z_ref):
        c, s1, s2 = c_ref[...], s1_ref[...], s2_ref[...]
        for h in range(n_heads):
            sl = slice(h * HEAD_W, (h + 1) * HEAD_W)
            yield sl, _rope(z_ref[:, sl], c, s1, s2)

    @pl.when(j == 0)
    def _():
        x = x_ref[...]
        ms = jnp.mean(x * x, axis=-1, keepdims=True)
        xn_ref[...] = (x * lax.rsqrt(ms + EPS) * g_ref[...]).astype(BF16)
        za_ref[...] = matmul()

    @pl.when(j == 1)
    def _():
        zb_ref[...] = matmul()
        for sl, qr in roped(za_ref):
            q_ref[:, sl] = (qr * (QK_DIM ** -0.5)).astype(BF16)

    @pl.when(j == 2)
    def _():
        za_ref[...] = matmul()
        for sl, kr in roped(zb_ref):
            k_ref[:, sl] = kr
            kb_ref[:, sl] = kr.astype(BF16)

    @pl.when(j == 3)
    def _():
        zb_ref[...] = matmul()
        v = za_ref[...]
        v_ref[...] = v
        vb_ref[...] = v.astype(BF16)

    @pl.when(j == 4)
    def _():
        u_ref[...] = zb_ref[...] * (1.0 / (1.0 + jnp.exp(-matmul())))


def _inproj(x, g, w, tabs, tm, n_pos_blocks):
    m, d = x.shape
    nw = w.shape[1] // 5
    emit_w = w.dtype != BF16
    row = lambda i, j: (i, 0)
    tab = lambda i, j: (i % n_pos_blocks, 0)
    w_col = lambda i, j: (0, j)
    out_f = jax.ShapeDtypeStruct((m, nw), F32)
    out_b = jax.ShapeDtypeStruct((m, nw), BF16)
    out_specs = [pl.BlockSpec((tm, nw), row)] * 6
    out_shape = [out_b, out_f, out_b, out_f, out_b, out_f]
    if emit_w:
        out_specs = [pl.BlockSpec((d, nw), w_col)] + out_specs
        out_shape = [jax.ShapeDtypeStruct(w.shape, BF16)] + out_shape
    return pl.pallas_call(
        functools.partial(_inproj_kernel, emit_w=emit_w),
        grid=(m // tm, 5),
        in_specs=[pl.BlockSpec((tm, d), row),
                  pl.BlockSpec((1, d), lambda i, j: (0, 0)),
                  pl.BlockSpec((d, nw), w_col),
                  pl.BlockSpec((tm, LANES), tab),
                  pl.BlockSpec((tm, LANES), tab),
                  pl.BlockSpec((tm, LANES), tab)],
        out_specs=out_specs,
        out_shape=out_shape,
        scratch_shapes=[pltpu.VMEM((tm, d), BF16), pltpu.VMEM((tm, nw), F32),
                        pltpu.VMEM((tm, nw), F32)],
        compiler_params=_cparams(("parallel", "arbitrary")),
        name="inproj",
    )(x, g, w, *tabs)


def _rope_tables(pos):
    pos = np.asarray(pos, np.float64)
    half = ROPE_DIM // 2
    inv = np.power(ROPE_THETA, -np.arange(half, dtype=np.float64) * 2.0 / ROPE_DIM)
    ang = pos[:, None] * inv[None, :]
    cos, sin = np.cos(ang), np.sin(ang)
    t = pos.shape[0]
    one = np.ones((t, QK_DIM - ROPE_DIM))
    zero = np.zeros((t, QK_DIM - ROPE_DIM))
    zh = np.zeros((t, half))
    c = np.concatenate([cos, cos, one], axis=-1)
    s1 = np.concatenate([-sin, zh, zero], axis=-1)
    s2 = np.concatenate([zh, sin, zero], axis=-1)
    dup = lambda a: jnp.asarray(np.concatenate([a, a], axis=-1), F32)
    return dup(c), dup(s1), dup(s2)


def _lam(lq1, lk1, lq2, lk2, lam_init):
    a = jnp.sum(lq1[...] * lk1[...], axis=-1, keepdims=True)
    b = jnp.sum(lq2[...] * lk2[...], axis=-1, keepdims=True)
    return jnp.exp(a) - jnp.exp(b) + lam_init


def _subln(o, g, lam_init):
    ms = jnp.mean(o * o, axis=-1, keepdims=True)
    return (o * lax.rsqrt(ms + EPS) * g) * (1.0 - lam_init)


def _fold_lanes(x, op):
    f = x[:, 0:LANES]
    for t in range(1, x.shape[1] // LANES):
        f = op(f, x[:, t * LANES:(t + 1) * LANES])
    return f


ATTN_SEG = 3


def _attn_kernel(pt_ref, lq1, lk1, lq2, lk2, sg_ref, qlo_ref, qhi_ref, k_ref, v_ref,
                 qs_ref, bias_ref, nbias_ref, kn_ref, vn_ref, *rest, tq, nq, n_pp, lam_init):
    del pt_ref
    pk_refs = rest[:n_pp]
    pv_refs = rest[n_pp:2 * n_pp]
    o_ref, os_ref = rest[2 * n_pp:2 * n_pp + 2]
    qq_lo, qq_hi, s_lo, s_hi, m_ref, l_ref, acc_ref = rest[2 * n_pp + 2:]
    i = pl.program_id(2)
    lane = lax.broadcasted_iota(jnp.int32, (tq, HEAD_W), 1)
    for q_ref, qq_ref in ((qlo_ref, qq_lo), (qhi_ref, qq_hi)):
        q = q_ref[...].astype(F32)
        qq_ref[0:tq, :] = jnp.where(lane < QK_DIM, q, 0.0).astype(BF16)
        qq_ref[tq:2 * tq, :] = jnp.where(lane >= QK_DIM, q, 0.0).astype(BF16)

    def segment(qq_ref, s_ref, blocks, diag):
        mx = None
        for c in blocks:
            cols = slice(c * tq, (c + 1) * tq)
            s = lax.dot_general(qq_ref[...], k_ref[cols, :], (((1,), (1,)), ((), ())),
                                preferred_element_type=F32)
            if c == diag:
                r = lax.broadcasted_iota(jnp.int32, s.shape, 0)
                col = lax.broadcasted_iota(jnp.int32, s.shape, 1)
                r = jnp.where(r >= tq, r - tq, r)
                s = jnp.where(col <= r, s, NEG)
            s_ref[:, cols] = s
            f = _fold_lanes(s, jnp.maximum)
            mx = f if mx is None else jnp.maximum(mx, f)
        m = jnp.max(mx, axis=-1, keepdims=True)
        ls = None
        pv = None
        for c in blocks:
            cols = slice(c * tq, (c + 1) * tq)
            p = jnp.exp(s_ref[:, cols] - m)
            f = _fold_lanes(p, jnp.add)
            ls = f if ls is None else ls + f
            d = jnp.dot(p.astype(BF16), v_ref[cols, :], preferred_element_type=F32)
            pv = d if pv is None else pv + d
        return m, ls, pv

    def attend(qq_ref, s_ref, n_blk):
        parts = [segment(qq_ref, s_ref, range(c0, min(c0 + ATTN_SEG, n_blk)), n_blk - 1)
                 for c0 in range(0, n_blk, ATTN_SEG)]
        m = parts[0][0]
        for mp, _, _ in parts[1:]:
            m = jnp.maximum(m, mp)
        ls = None
        pv = None
        for mp, lsp, pvp in parts:
            w = jnp.exp(mp - m)
            ls = w * lsp if ls is None else ls + w * lsp
            pv = w * pvp if pv is None else pv + w * pvp
        accn = pv / jnp.sum(ls, axis=-1, keepdims=True)
        lam = _lam(lq1, lk1, lq2, lk2, lam_init)
        o = accn[0:tq, :] - lam * accn[tq:2 * tq, :]
        return _subln(o, sg_ref[...], lam_init).astype(BF16)

    def paged_update(kbs, vbs, bias):
        qs = qs_ref[...]
        ss = [lax.dot_general(qs, kb, (((1,), (1,)), ((), ())), preferred_element_type=F32) + bias
              for kb in kbs]
        mx = _fold_lanes(ss[0], jnp.maximum)
        for s in ss[1:]:
            mx = jnp.maximum(mx, _fold_lanes(s, jnp.maximum))
        m_prev = m_ref[...]
        m_new = jnp.maximum(m_prev, jnp.max(mx, axis=-1, keepdims=True))
        alpha = jnp.exp(m_prev - m_new)
        ls = None
        pv = None
        for s, vb in zip(ss, vbs):
            p = jnp.exp(s - m_new)
            f = _fold_lanes(p, jnp.add)
            ls = f if ls is None else ls + f
            d = jnp.dot(p.astype(BF16), vb, preferred_element_type=F32)
            pv = d if pv is None else pv + d
        l_ref[...] = alpha * l_ref[...] + jnp.sum(ls, axis=-1, keepdims=True)
        acc_ref[...] = alpha * acc_ref[...] + pv
        m_ref[...] = m_new

    def paged(step):
        if step == 0:
            m_ref[...] = jnp.full(m_ref.shape, NEG, F32)
            l_ref[...] = jnp.zeros(l_ref.shape, F32)
            acc_ref[...] = jnp.zeros(acc_ref.shape, F32)
            paged_update([kn_ref[...]], [vn_ref[...]], nbias_ref[...])
        bias = jnp.concatenate([bias_ref[...]] * (pk_refs[0].shape[0] // LANES), axis=1)
        paged_update([r[...].astype(BF16) for r in pk_refs],
                     [r[...].astype(BF16) for r in pv_refs], bias)
        if step == nq // 2 - 1:
            accn = acc_ref[...] / l_ref[...]
            half = accn.shape[0] // 2
            lam = _lam(lq1, lk1, lq2, lk2, lam_init)
            o = accn[0:half, :] - lam * accn[half:, :]
            os_ref[...] = _subln(o, sg_ref[...], lam_init).astype(BF16)

    def variant(step):
        paged(step)
        o_ref[0:tq, :] = attend(qq_lo, s_lo, step + 1)
        o_ref[tq:2 * tq, :] = attend(qq_hi, s_hi, nq - step)

    for n in range(nq // 2):
        pl.when(i == n)(functools.partial(variant, n))


def _paired_block(r, nq):
    b, rb = r // nq, r % nq
    return b * nq + jnp.where(rb < nq // 2, 2 * rb, 2 * (nq - 1 - rb) + 1)


def _attention(q_bf, k_bf, v_bf, q_s, kn_bf, vn_bf, cache_k, cache_v, page_table, lams, subln_g,
               batch, seq, dec_batch, dec_seq, lam_init, tq):
    m, w = q_bf.shape
    n_heads = w // HEAD_W
    nq = seq // tq
    half = nq // 2
    n_pool, page, _, _ = cache_k.shape
    n_pages = page_table.shape[1]
    n_pp = n_pages // half
    assert nq % 2 == 0 and n_pp * half == n_pages and batch * n_heads == dec_batch
    rows_q = dec_seq * n_heads
    page_rows = page * n_heads
    q4 = q_s.reshape(dec_batch, dec_seq, n_heads, HEAD_W).transpose(0, 2, 1, 3)
    q4 = q4.reshape(dec_batch, rows_q, HEAD_W)
    lane = jnp.arange(HEAD_W)
    qt = jnp.concatenate([jnp.where(lane < QK_DIM, q4, 0), jnp.where(lane >= QK_DIM, q4, 0)],
                         axis=1).astype(BF16)
    pad = lambda a: jnp.pad(a.reshape(dec_batch, rows_q, HEAD_W),
                            ((0, 0), (0, LANES - rows_q), (0, 0)))
    kn, vn = pad(kn_bf), pad(vn_bf)
    c = np.arange(2 * rows_q)
    c_head = (c % rows_q) // dec_seq
    c_query = c % dec_seq
    assert LANES % n_heads == 0
    r = np.arange(LANES)
    bias = jnp.asarray(np.where((r % n_heads)[None, :] == c_head[:, None], 0.0, NEG), F32)
    rn = np.arange(LANES)
    ok = ((rn % n_heads)[None, :] == c_head[:, None]) & ((rn // n_heads)[None, :] <= c_query[:, None]) \
        & (rn < rows_q)[None, :]
    nbias = jnp.asarray(np.where(ok, 0.0, NEG), F32)
    ck = cache_k.reshape(n_pool, page_rows, HEAD_W)
    cv = cache_v.reshape(n_pool, page_rows, HEAD_W)

    small = lambda b, h, i, pt: (0, 0)
    kv = lambda b, h, i, pt: (b, h)
    per_entry = lambda b, h, i, pt: (b * n_heads + h, 0, 0)

    def page_spec(t):
        return pl.BlockSpec((None, page_rows, HEAD_W),
                            lambda b, h, i, pt: (pt[b * n_heads + h, i * n_pp + t], 0, 0))

    grid_spec = pltpu.PrefetchScalarGridSpec(
        num_scalar_prefetch=1,
        grid=(batch, n_heads, half),
        in_specs=[pl.BlockSpec((1, QK_DIM), small)] * 4
                 + [pl.BlockSpec((1, HEAD_W), small),
                    pl.BlockSpec((tq, HEAD_W), lambda b, h, i, pt: (b * nq + i, h)),
                    pl.BlockSpec((tq, HEAD_W), lambda b, h, i, pt: (b * nq + nq - 1 - i, h)),
                    pl.BlockSpec((seq, HEAD_W), kv),
                    pl.BlockSpec((seq, HEAD_W), kv),
                    pl.BlockSpec((None, 2 * rows_q, HEAD_W), per_entry),
                    pl.BlockSpec((2 * rows_q, LANES), small),
                    pl.BlockSpec((2 * rows_q, LANES), small),
                    pl.BlockSpec((None, LANES, HEAD_W), per_entry),
                    pl.BlockSpec((None, LANES, HEAD_W), per_entry)]
                 + [page_spec(t) for t in range(n_pp)] * 2,
        out_specs=[pl.BlockSpec((2 * tq, HEAD_W), lambda b, h, i, pt: (b * half + i, h)),
                   pl.BlockSpec((None, rows_q, HEAD_W), per_entry)],
        scratch_shapes=[pltpu.VMEM((2 * tq, HEAD_W), BF16),
                        pltpu.VMEM((2 * tq, HEAD_W), BF16),
                        pltpu.VMEM((2 * tq, half * tq), F32),
                        pltpu.VMEM((2 * tq, seq), F32),
                        pltpu.VMEM((2 * rows_q, 1), F32),
                        pltpu.VMEM((2 * rows_q, 1), F32),
                        pltpu.VMEM((2 * rows_q, HEAD_W), F32)],
    )
    o_p, o_s = pl.pallas_call(
        functools.partial(_attn_kernel, tq=tq, nq=nq, n_pp=n_pp, lam_init=lam_init),
        grid_spec=grid_spec,
        out_shape=[jax.ShapeDtypeStruct((m, w), BF16),
                   jax.ShapeDtypeStruct((dec_batch, rows_q, HEAD_W), BF16)],
        compiler_params=_cparams(("arbitrary", "arbitrary", "arbitrary")),
        name="attention",
    )(page_table, *lams, subln_g, q_bf, q_bf, k_bf, v_bf, qt, bias, nbias, kn, vn,
      *([ck] * n_pp), *([cv] * n_pp))
    o_s = o_s.reshape(dec_batch, n_heads, dec_seq, HEAD_W).transpose(0, 2, 1, 3)
    return o_p, o_s.reshape(dec_batch * dec_seq, n_heads * HEAD_W)


def _ln_swish(y, g, b):
    mu = jnp.mean(y, axis=-1, keepdims=True)
    yc = y - mu
    var = jnp.mean(yc * yc, axis=-1, keepdims=True)
    z = yc * lax.rsqrt(var + EPS) * g + b
    return z * (1.0 / (1.0 + jnp.exp(-z)))


CONV_HALO = 32
CONV_CHUNK = 16


def _conv_block(t, cur_ref, tail_ref, w_ref, cb_ref, g_ref, b_ref, ext_ref, y_ref, tt):
    n_slab = cur_ref.shape[1] // LANES
    for sl in range(n_slab):
        lanes = slice(sl * LANES, (sl + 1) * LANES)
        tail = tail_ref[:, lanes]
        ext_ref[sl, 0:CONV_HALO, :] = jnp.where(t == 0, jnp.zeros_like(tail), tail)
        ext_ref[sl, CONV_HALO:CONV_HALO + tt, :] = cur_ref[:, lanes]
    off = CONV_HALO - (CONV_K - 1)
    n_ch = cur_ref.shape[1]

    half = CONV_CHUNK // 2
    for base in range(0, tt, CONV_CHUNK):
        for sl in range(n_slab):
            lanes = slice(sl * LANES, (sl + 1) * LANES)
            xs = [ext_ref[sl, pl.ds(base + off + k, half, stride=2), :] for k in range(CONV_K + 1)]
            even = jnp.zeros((half, LANES), F32)
            odd = jnp.zeros((half, LANES), F32)
            for j in range(CONV_K):
                w = w_ref[j:j + 1, lanes]
                even = even + xs[j] * w
                odd = odd + xs[j + 1] * w
            y_ref[sl, pl.ds(base, half, stride=2), :] = even + cb_ref[:, lanes]
            y_ref[sl, pl.ds(base + 1, half, stride=2), :] = odd + cb_ref[:, lanes]

    tot = y_ref[0]
    for sl in range(1, n_slab):
        tot = tot + y_ref[sl]
    mu = jnp.sum(tot, axis=-1, keepdims=True) * (1.0 / n_ch)
    sq = None
    for sl in range(n_slab):
        yc = y_ref[sl] - mu
        sq = yc * yc if sq is None else sq + yc * yc
    rstd = lax.rsqrt(jnp.sum(sq, axis=-1, keepdims=True) * (1.0 / n_ch) + EPS)
    out = []
    for sl in range(n_slab):
        lanes = slice(sl * LANES, (sl + 1) * LANES)
        z = (y_ref[sl] - mu) * rstd * g_ref[:, lanes] + b_ref[:, lanes]
        out.append(z * (1.0 / (1.0 + jnp.exp(-z))))
    return out


def _conv_sample_kernel(ext_ref, w_ref, cb_ref, g_ref, b_ref, o_ref):
    nb, rows, _ = o_ref.shape
    for e in range(nb):
        acc = jnp.zeros(o_ref.shape[1:], F32)
        for j in range(CONV_K):
            acc = acc + ext_ref[e, j:j + rows, :] * w_ref[j:j + 1, :]
        y = acc + cb_ref[...]
        o_ref[e] = _ln_swish(y, g_ref[...], b_ref[...]).astype(BF16)


CONV_SAMPLE_ROWS = 8
CONV_SAMPLE_ENTRIES = 8


def _conv_sample(ext, conv_w, conv_b, ln_g, ln_b, dec_seq):
    nb, n_rows, c = ext.shape
    assert dec_seq <= CONV_SAMPLE_ROWS and nb % CONV_SAMPLE_ENTRIES == 0
    pad_rows = (CONV_K - 1) + CONV_SAMPLE_ROWS
    ext_p = jnp.pad(ext, ((0, 0), (0, pad_rows - n_rows), (0, 0)))
    small = lambda b: (0, 0)
    o = pl.pallas_call(
        _conv_sample_kernel,
        grid=(nb // CONV_SAMPLE_ENTRIES,),
        in_specs=[pl.BlockSpec((CONV_SAMPLE_ENTRIES, pad_rows, c), lambda b: (b, 0, 0)),
                  pl.BlockSpec((CONV_K, c), small),
                  pl.BlockSpec((1, c), small),
                  pl.BlockSpec((1, c), small),
                  pl.BlockSpec((1, c), small)],
        out_specs=pl.BlockSpec((CONV_SAMPLE_ENTRIES, CONV_SAMPLE_ROWS, c), lambda b: (b, 0, 0)),
        out_shape=jax.ShapeDtypeStruct((nb, CONV_SAMPLE_ROWS, c), BF16),
        compiler_params=_cparams(("parallel",)),
        name="conv_sample",
    )(ext_p, conv_w, conv_b, ln_g, ln_b)
    return o[:, :dec_seq].reshape(nb * dec_seq, c)


def _outproj_kernel(o_ref, c_ref, x_ref, wa_ref, wc_ref, g_ref, h_ref, hn_ref):
    h = x_ref[...] + jnp.dot(o_ref[...], wa_ref[...], preferred_element_type=F32) \
        + jnp.dot(c_ref[...], wc_ref[...], preferred_element_type=F32)
    h_ref[...] = h
    ms = jnp.mean(h * h, axis=-1, keepdims=True)
    hn_ref[...] = (h * lax.rsqrt(ms + EPS) * g_ref[...]).astype(BF16)


def _outproj(o_n, c, x, w_out_bf, g2, tm):
    m, d = x.shape
    wa = o_n.shape[1]
    row = lambda i: (i, 0)
    return pl.pallas_call(
        _outproj_kernel,
        grid=(m // tm,),
        in_specs=[pl.BlockSpec((tm, wa), row),
                  pl.BlockSpec((tm, c.shape[1]), row),
                  pl.BlockSpec((tm, d), row),
                  pl.BlockSpec((wa, d), lambda i: (0, 0)),
                  pl.BlockSpec((c.shape[1], d), lambda i: (wa // c.shape[1], 0)),
                  pl.BlockSpec((1, d), lambda i: (0, 0))],
        out_specs=[pl.BlockSpec((tm, d), row), pl.BlockSpec((tm, d), row)],
        out_shape=[jax.ShapeDtypeStruct((m, d), F32), jax.ShapeDtypeStruct((m, d), BF16)],
        compiler_params=_cparams(("parallel",)),
        name="outproj",
    )(o_n, c, x, w_out_bf, w_out_bf, g2)


def _outproj_conv_kernel(o_ref, cur_ref, tail_ref, x_ref, wa_ref, wc_ref, cw_ref, cb_ref, lg_ref,
                         lb_ref, g_ref, h_ref, hn_ref, ext_ref, y_ref, *, tt):
    slabs = _conv_block(pl.program_id(1), cur_ref, tail_ref, cw_ref, cb_ref, lg_ref, lb_ref,
                        ext_ref, y_ref, tt)
    c = jnp.concatenate([s.astype(BF16) for s in slabs], axis=1)
    h = x_ref[...] + jnp.dot(o_ref[...], wa_ref[...], preferred_element_type=F32) \
        + jnp.dot(c, wc_ref[...], preferred_element_type=F32)
    h_ref[...] = h
    ms = jnp.mean(h * h, axis=-1, keepdims=True)
    hn_ref[...] = (h * lax.rsqrt(ms + EPS) * g_ref[...]).astype(BF16)


def _outproj_conv(o_n, u, x, w_out_bf, conv_w, conv_b, ln_g, ln_b, g2, batch, seq, tt, o_block):
    m, d = x.shape
    wa = o_n.shape[1]
    c = u.shape[1]
    nt = seq // tt
    ratio = tt // CONV_HALO
    row = lambda b, t: (b * nt + t, 0)
    tail = lambda b, t: (jnp.maximum((b * nt + t) * ratio - 1, 0), 0)
    small = lambda b, t: (0, 0)
    return pl.pallas_call(
        functools.partial(_outproj_conv_kernel, tt=tt),
        grid=(batch, nt),
        in_specs=[pl.BlockSpec((tt, wa), lambda b, t: (o_block(b * nt + t), 0)),
                  pl.BlockSpec((tt, c), row),
                  pl.BlockSpec((CONV_HALO, c), tail),
                  pl.BlockSpec((tt, d), row),
                  pl.BlockSpec((wa, d), small),
                  pl.BlockSpec((c, d), lambda b, t: (wa // c, 0)),
                  pl.BlockSpec((CONV_K, c), small),
                  pl.BlockSpec((1, c), small),
                  pl.BlockSpec((1, c), small),
                  pl.BlockSpec((1, c), small),
                  pl.BlockSpec((1, d), small)],
        out_specs=[pl.BlockSpec((tt, d), row), pl.BlockSpec((tt, d), row)],
        out_shape=[jax.ShapeDtypeStruct((m, d), F32), jax.ShapeDtypeStruct((m, d), BF16)],
        scratch_shapes=[pltpu.VMEM((c // LANES, CONV_HALO + tt, LANES), F32),
                        pltpu.VMEM((c // LANES, tt, LANES), F32)],
        compiler_params=_cparams(("parallel", "parallel")),
        name="outproj_conv",
    )(o_n, u, u, x, w_out_bf, w_out_bf, conv_w, conv_b, ln_g, ln_b, g2)


def _ffn_kernel(hn_ref, h_ref, wu_ref, wd_ref, g_ref, *refs, emit_w):
    if emit_w:
        wub_ref, wdb_ref, refs = refs[0], refs[1], refs[2:]
    y_ref, acc_ref = refs
    f = pl.program_id(1)

    @pl.when(f == 0)
    def _():
        acc_ref[...] = jnp.zeros(acc_ref.shape, F32)

    wu, wd = wu_ref[...], wd_ref[...]
    if emit_w:
        wu, wd = wu.astype(BF16), wd.astype(BF16)
        wub_ref[...] = wu
        wdb_ref[...] = wd
    a = jnp.maximum(jnp.dot(hn_ref[...], wu, preferred_element_type=F32), 0.0)
    acc_ref[...] += jnp.dot((a * a).astype(BF16), wd, preferred_element_type=F32)

    @pl.when(f == pl.num_programs(1) - 1)
    def _():
        y = h_ref[...] + acc_ref[...]
        ms = jnp.mean(y * y, axis=-1, keepdims=True)
        y_ref[...] = y * lax.rsqrt(ms + EPS) * g_ref[...]


def _ffn(hn, h, w_up, w_down, final_g, tm, tf):
    m, d = h.shape
    dff = w_up.shape[1]
    emit_w = w_up.dtype != BF16
    assert not emit_w or m == tm, "each weight tile must be visited exactly once to emit it"
    row = lambda i, f: (i, 0)
    up = lambda i, f: (0, f)
    down = lambda i, f: (f, 0)
    out_specs = [pl.BlockSpec((tm, d), row)]
    out_shape = [jax.ShapeDtypeStruct((m, d), F32)]
    if emit_w:
        out_specs = [pl.BlockSpec((d, tf), up), pl.BlockSpec((tf, d), down)] + out_specs
        out_shape = [jax.ShapeDtypeStruct(w_up.shape, BF16),
                     jax.ShapeDtypeStruct(w_down.shape, BF16)] + out_shape
    out = pl.pallas_call(
        functools.partial(_ffn_kernel, emit_w=emit_w),
        grid=(m // tm, dff // tf),
        in_specs=[pl.BlockSpec((tm, d), row),
                  pl.BlockSpec((tm, d), row),
                  pl.BlockSpec((d, tf), up),
                  pl.BlockSpec((tf, d), down),
                  pl.BlockSpec((1, d), lambda i, f: (0, 0))],
        out_specs=out_specs,
        out_shape=out_shape,
        scratch_shapes=[pltpu.VMEM((tm, d), F32)],
        compiler_params=_cparams(("parallel", "arbitrary")),
        name="ffn",
    )(hn, h, w_up, w_down, final_g)
    return out if emit_w else out[0]


def kernel(x_prompt, x_sample, cache_k, cache_v, state_conv, page_table, norm1_g, w_in,
           lambda_q1, lambda_k1, lambda_q2, lambda_k2, subln_g, conv_w, conv_b,
           conv_ln_g, conv_ln_b, w_out, norm2_g, w_up, w_down, final_g):
    depth = w_in.shape[0]
    assert depth == 1, "final norm is fused into the (single) layer's FFN kernel"
    batch, seq, d = x_prompt.shape
    dec_batch, dec_seq, _ = x_sample.shape
    n_heads, head_w = cache_k.shape[3], cache_k.shape[4]
    assert head_w == HEAD_W
    past = page_table.shape[1] * cache_k.shape[2]
    conv_c = conv_w.shape[2]

    xp = x_prompt.reshape(batch * seq, d)
    xs = x_sample.reshape(dec_batch * dec_seq, d)
    tm_s = dec_batch * dec_seq

    tabs_p = _rope_tables(np.arange(seq))
    tabs_s = _rope_tables(np.tile(past + np.arange(dec_seq), dec_batch))

    l = 0
    lam_init = 0.8 - 0.6 * math.exp(-0.3 * l)
    row = lambda a: a[l].reshape(1, -1)
    lams = (row(lambda_q1), row(lambda_k1), row(lambda_q2), row(lambda_k2))
    g1, g2, sg = row(norm1_g), row(norm2_g), row(subln_g)
    cb, lg, lb = row(conv_b), row(conv_ln_g), row(conv_ln_b)
    fg = final_g.reshape(1, -1)
    w_out_bf = w_out[l].astype(BF16)
    cw = conv_w[l]

    w_in_bf, q_s, k_s, kb_s, v_s, vb_s, u_s = _inproj(xs, g1, w_in[l], tabs_s, tm_s, 1)
    q_p, k_p, kb_p, v_p, vb_p, u_p = _inproj(xp, g1, w_in_bf, tabs_p, TM_DENSE, seq // TM_DENSE)
    o_p, o_s = _attention(q_p, kb_p, vb_p, q_s, kb_s, vb_s, cache_k[l], cache_v[l], page_table,
                          lams, sg, batch, seq, dec_batch, dec_seq, lam_init, TQ)

    ext_s = jnp.concatenate([state_conv[l], u_s.reshape(dec_batch, dec_seq, conv_c)], axis=1)
    c_s = _conv_sample(ext_s, cw, cb, lg, lb, dec_seq)
    h_s, hn_s = _outproj(o_s, c_s, xs, w_out_bf, g2, tm_s)
    w_up_bf, w_down_bf, y_s = _ffn(hn_s, h_s, w_up[l], w_down[l], fg, tm_s, TF_SAMPLE)

    h_p, hn_p = _outproj_conv(o_p, u_p, xp, w_out_bf, cw, cb, lg, lb, g2, batch, seq, TQ,
                              functools.partial(_paired_block, nq=seq // TQ))
    y_p = _ffn(hn_p, h_p, w_up_bf, w_down_bf, fg, TM_DENSE, TF_PROMPT)

    keep = CONV_K - 1
    return (y_p.reshape(batch, seq, d),
            y_s.reshape(dec_batch, dec_seq, d),
            k_p.reshape(1, batch, seq, n_heads, head_w),
            v_p.reshape(1, batch, seq, n_heads, head_w),
            u_p.reshape(batch, seq, conv_c)[:, seq - keep:][None],
            k_s.reshape(1, dec_batch, dec_seq, n_heads, head_w),
            v_s.reshape(1, dec_batch, dec_seq, n_heads, head_w),
            ext_s[:, dec_seq:][None])
```

```python
import functools
import math

import numpy as np
import jax
import jax.numpy as jnp
from jax import lax
from jax.experimental import pallas as pl
from jax.experimental.pallas import tpu as pltpu

F32 = jnp.float32
BF16 = jnp.bfloat16

EPS = 1e-6
ROPE_THETA = 500000.0
LANES = 128
HEAD_W = 128
QK_DIM = 64
ROPE_DIM = 16
CONV_K = 31
NEG = -1e30
VMEM_LIMIT = 56 * 1024 * 1024

TM_DENSE = 512
TQ = 256
TF_PROMPT = 1024
TF_SAMPLE = 512


def _cparams(sem):
    return pltpu.CompilerParams(dimension_semantics=sem, vmem_limit_bytes=VMEM_LIMIT)


def _rope(z, c, s1, s2):
    return z * c + pltpu.roll(z, LANES - 8, 1) * s1 + pltpu.roll(z, 8, 1) * s2


def _inproj_kernel(x_ref, g_ref, w_ref, c_ref, s1_ref, s2_ref, *refs, emit_w):
    if emit_w:
        wb_ref, refs = refs[0], refs[1:]
    q_ref, k_ref, kb_ref, v_ref, vb_ref, u_ref, xn_ref, za_ref, zb_ref = refs
    j = pl.program_id(1)
    r = pl.program_id(2)
    n_heads = za_ref.shape[2] // HEAD_W

    def matmul():
        w = w_ref[...]
        if emit_w:
            w = w.astype(BF16)
            wb_ref[...] = w
        return jnp.dot(xn_ref[r], w, preferred_element_type=F32)

    def roped(z_ref):
        c, s1, s2 = c_ref[...], s1_ref[...], s2_ref[...]
        for h in range(n_heads):
            sl = slice(h * HEAD_W, (h + 1) * HEAD_W)
            yield sl, _rope(z_ref[r, :, sl], c, s1, s2)

    @pl.when(j == 0)
    def _():
        x = x_ref[...]
        ms = jnp.mean(x * x, axis=-1, keepdims=True)
        xn_ref[r] = (x * lax.rsqrt(ms + EPS) * g_ref[...]).astype(BF16)
        za_ref[r] = matmul()

    @pl.when(j == 1)
    def _():
        zb_ref[r] = matmul()
        for sl, qr in roped(za_ref):
            q_ref[:, sl] = (qr * (QK_DIM ** -0.5)).astype(BF16)

    @pl.when(j == 2)
    def _():
        za_ref[r] = matmul()
        for sl, kr in roped(zb_ref):
            k_ref[:, sl] = kr
            kb_ref[:, sl] = kr.astype(BF16)

    @pl.when(j == 3)
    def _():
        zb_ref[r] = matmul()
        v = za_ref[r]
        v_ref[...] = v
        vb_ref[...] = v.astype(BF16)

    @pl.when(j == 4)
    def _():
        u_ref[...] = zb_ref[r] * (1.0 / (1.0 + jnp.exp(-matmul())))


def _inproj(x, g, w, tabs, tm, n_pos_blocks):
    m, d = x.shape
    nw = w.shape[1] // 5
    emit_w = w.dtype != BF16
    assert m % (2 * tm) == 0
    blk = lambda i, r: 2 * i + r
    x_row = lambda i, j, r: (blk(i, jnp.where(j == 0, r, 1)), 0)
    tab = lambda i, j, r: (blk(i, r) % n_pos_blocks, 0)
    w_col = lambda i, j, r: (0, j)

    def written_at(step):
        return lambda i, j, r: (blk(i, ((j > step) | ((j == step) & (r == 1))).astype(jnp.int32)), 0)

    out_f = jax.ShapeDtypeStruct((m, nw), F32)
    out_b = jax.ShapeDtypeStruct((m, nw), BF16)
    out_specs = [pl.BlockSpec((tm, nw), written_at(s)) for s in (1, 2, 2, 3, 3, 4)]
    out_shape = [out_b, out_f, out_b, out_f, out_b, out_f]
    if emit_w:
        out_specs = [pl.BlockSpec((d, nw), w_col)] + out_specs
        out_shape = [jax.ShapeDtypeStruct(w.shape, BF16)] + out_shape
    return pl.pallas_call(
        functools.partial(_inproj_kernel, emit_w=emit_w),
        grid=(m // (2 * tm), 5, 2),
        in_specs=[pl.BlockSpec((tm, d), x_row),
                  pl.BlockSpec((1, d), lambda i, j, r: (0, 0)),
                  pl.BlockSpec((d, nw), w_col),
                  pl.BlockSpec((tm, LANES), tab),
                  pl.BlockSpec((tm, LANES), tab),
                  pl.BlockSpec((tm, LANES), tab)],
        out_specs=out_specs,
        out_shape=out_shape,
        scratch_shapes=[pltpu.VMEM((2, tm, d), BF16), pltpu.VMEM((2, tm, nw), F32),
                        pltpu.VMEM((2, tm, nw), F32)],
        compiler_params=_cparams(("arbitrary", "arbitrary", "arbitrary")),
        name="inproj",
    )(x, g, w, *tabs)


def _rope_tables(pos):
    pos = np.asarray(pos, np.float64)
    half = ROPE_DIM // 2
    inv = np.power(ROPE_THETA, -np.arange(half, dtype=np.float64) * 2.0 / ROPE_DIM)
    ang = pos[:, None] * inv[None, :]
    cos, sin = np.cos(ang), np.sin(ang)
    t = pos.shape[0]
    one = np.ones((t, QK_DIM - ROPE_DIM))
    zero = np.zeros((t, QK_DIM - ROPE_DIM))
    zh = np.zeros((t, half))
    c = np.concatenate([cos, cos, one], axis=-1)
    s1 = np.concatenate([-sin, zh, zero], axis=-1)
    s2 = np.concatenate([zh, sin, zero], axis=-1)
    dup = lambda a: jnp.asarray(np.concatenate([a, a], axis=-1), F32)
    return dup(c), dup(s1), dup(s2)


def _lam(lq1, lk1, lq2, lk2, lam_init):
    a = jnp.sum(lq1[...] * lk1[...], axis=-1, keepdims=True)
    b = jnp.sum(lq2[...] * lk2[...], axis=-1, keepdims=True)
    return jnp.exp(a) - jnp.exp(b) + lam_init


def _subln(o, g, lam_init):
    ms = jnp.mean(o * o, axis=-1, keepdims=True)
    return (o * lax.rsqrt(ms + EPS) * g) * (1.0 - lam_init)


def _fold_lanes(x, op):
    f = x[:, 0:LANES]
    for t in range(1, x.shape[1] // LANES):
        f = op(f, x[:, t * LANES:(t + 1) * LANES])
    return f


ATTN_SEG = 3


def _attn_kernel(pt_ref, lq1, lk1, lq2, lk2, sg_ref, qlo_ref, qhi_ref, k_ref, v_ref,
                 qs_ref, bias_ref, nbias_ref, kn_ref, vn_ref, *rest, tq, nq, n_pp, lam_init):
    del pt_ref
    pk_refs = rest[:n_pp]
    pv_refs = rest[n_pp:2 * n_pp]
    o_ref, os_ref = rest[2 * n_pp:2 * n_pp + 2]
    qq_lo, qq_hi, s_lo, s_hi, m_ref, l_ref, acc_ref = rest[2 * n_pp + 2:]
    i = pl.program_id(2)
    lane = lax.broadcasted_iota(jnp.int32, (tq, HEAD_W), 1)
    for q_ref, qq_ref in ((qlo_ref, qq_lo), (qhi_ref, qq_hi)):
        q = q_ref[...].astype(F32)
        qq_ref[0:tq, :] = jnp.where(lane < QK_DIM, q, 0.0).astype(BF16)
        qq_ref[tq:2 * tq, :] = jnp.where(lane >= QK_DIM, q, 0.0).astype(BF16)

    def segment(qq_ref, s_ref, blocks, diag):
        mx = None
        for c in blocks:
            cols = slice(c * tq, (c + 1) * tq)
            s = lax.dot_general(qq_ref[...], k_ref[cols, :], (((1,), (1,)), ((), ())),
                                preferred_element_type=F32)
            if c == diag:
                r = lax.broadcasted_iota(jnp.int32, s.shape, 0)
                col = lax.broadcasted_iota(jnp.int32, s.shape, 1)
                r = jnp.where(r >= tq, r - tq, r)
                s = jnp.where(col <= r, s, NEG)
            s_ref[:, cols] = s
            f = _fold_lanes(s, jnp.maximum)
            mx = f if mx is None else jnp.maximum(mx, f)
        m = jnp.max(mx, axis=-1, keepdims=True)
        ls = None
        pv = None
        for c in blocks:
            cols = slice(c * tq, (c + 1) * tq)
            p = jnp.exp(s_ref[:, cols] - m)
            f = _fold_lanes(p, jnp.add)
            ls = f if ls is None else ls + f
            d = jnp.dot(p.astype(BF16), v_ref[cols, :], preferred_element_type=F32)
            pv = d if pv is None else pv + d
        return m, ls, pv

    def attend(qq_ref, s_ref, n_blk):
        parts = [segment(qq_ref, s_ref, range(c0, min(c0 + ATTN_SEG, n_blk)), n_blk - 1)
                 for c0 in range(0, n_blk, ATTN_SEG)]
        m = parts[0][0]
        for mp, _, _ in parts[1:]:
            m = jnp.maximum(m, mp)
        ls = None
        pv = None
        for mp, lsp, pvp in parts:
            w = jnp.exp(mp - m)
            ls = w * lsp if ls is None else ls + w * lsp
            pv = w * pvp if pv is None else pv + w * pvp
        accn = pv / jnp.sum(ls, axis=-1, keepdims=True)
        lam = _lam(lq1, lk1, lq2, lk2, lam_init)
        o = accn[0:tq, :] - lam * accn[tq:2 * tq, :]
        return _subln(o, sg_ref[...], lam_init).astype(BF16)

    def paged_update(kbs, vbs, bias):
        qs = qs_ref[...]
        ss = [lax.dot_general(qs, kb, (((1,), (1,)), ((), ())), preferred_element_type=F32) + bias
              for kb in kbs]
        mx = _fold_lanes(ss[0], jnp.maximum)
        for s in ss[1:]:
            mx = jnp.maximum(mx, _fold_lanes(s, jnp.maximum))
        m_prev = m_ref[...]
        m_new = jnp.maximum(m_prev, jnp.max(mx, axis=-1, keepdims=True))
        alpha = jnp.exp(m_prev - m_new)
        ls = None
        pv = None
        for s, vb in zip(ss, vbs):
            p = jnp.exp(s - m_new)
            f = _fold_lanes(p, jnp.add)
            ls = f if ls is None else ls + f
            d = jnp.dot(p.astype(BF16), vb, preferred_element_type=F32)
            pv = d if pv is None else pv + d
        l_ref[...] = alpha * l_ref[...] + jnp.sum(ls, axis=-1, keepdims=True)
        acc_ref[...] = alpha * acc_ref[...] + pv
        m_ref[...] = m_new

    def paged(step):
        if step == 0:
            m_ref[...] = jnp.full(m_ref.shape, NEG, F32)
            l_ref[...] = jnp.zeros(l_ref.shape, F32)
            acc_ref[...] = jnp.zeros(acc_ref.shape, F32)
            paged_update([kn_ref[...]], [vn_ref[...]], nbias_ref[...])
        bias = jnp.concatenate([bias_ref[...]] * (pk_refs[0].shape[0] // LANES), axis=1)
        paged_update([r[...].astype(BF16) for r in pk_refs],
                     [r[...].astype(BF16) for r in pv_refs], bias)
        if step == nq // 2 - 1:
            accn = acc_ref[...] / l_ref[...]
            half = accn.shape[0] // 2
            lam = _lam(lq1, lk1, lq2, lk2, lam_init)
            o = accn[0:half, :] - lam * accn[half:, :]
            os_ref[...] = _subln(o, sg_ref[...], lam_init).astype(BF16)

    def variant(step):
        paged(step)
        o_ref[0:tq, :] = attend(qq_lo, s_lo, step + 1)
        o_ref[tq:2 * tq, :] = attend(qq_hi, s_hi, nq - step)

    for n in range(nq // 2):
        pl.when(i == n)(functools.partial(variant, n))


def _paired_block(r, nq):
    b, rb = r // nq, r % nq
    return b * nq + jnp.where(rb < nq // 2, 2 * rb, 2 * (nq - 1 - rb) + 1)


def _attention(q_bf, k_bf, v_bf, q_s, kn_bf, vn_bf, cache_k, cache_v, page_table, lams, subln_g,
               batch, seq, dec_batch, dec_seq, lam_init, tq):
    m, w = q_bf.shape
    n_heads = w // HEAD_W
    nq = seq // tq
    half = nq // 2
    n_pool, page, _, _ = cache_k.shape
    n_pages = page_table.shape[1]
    n_pp = n_pages // half
    assert nq % 2 == 0 and n_pp * half == n_pages and batch * n_heads == dec_batch
    rows_q = dec_seq * n_heads
    page_rows = page * n_heads
    q4 = q_s.reshape(dec_batch, dec_seq, n_heads, HEAD_W).transpose(0, 2, 1, 3)
    q4 = q4.reshape(dec_batch, rows_q, HEAD_W)
    lane = jnp.arange(HEAD_W)
    qt = jnp.concatenate([jnp.where(lane < QK_DIM, q4, 0), jnp.where(lane >= QK_DIM, q4, 0)],
                         axis=1).astype(BF16)
    pad = lambda a: jnp.pad(a.reshape(dec_batch, rows_q, HEAD_W),
                            ((0, 0), (0, LANES - rows_q), (0, 0)))
    kn, vn = pad(kn_bf), pad(vn_bf)
    c = np.arange(2 * rows_q)
    c_head = (c % rows_q) // dec_seq
    c_query = c % dec_seq
    assert LANES % n_heads == 0
    r = np.arange(LANES)
    bias = jnp.asarray(np.where((r % n_heads)[None, :] == c_head[:, None], 0.0, NEG), F32)
    rn = np.arange(LANES)
    ok = ((rn % n_heads)[None, :] == c_head[:, None]) & ((rn // n_heads)[None, :] <= c_query[:, None]) \
        & (rn < rows_q)[None, :]
    nbias = jnp.asarray(np.where(ok, 0.0, NEG), F32)
    ck = cache_k.reshape(n_pool, page_rows, HEAD_W)
    cv = cache_v.reshape(n_pool, page_rows, HEAD_W)

    small = lambda b, h, i, pt: (0, 0)
    kv = lambda b, h, i, pt: (b, h)
    per_entry = lambda b, h, i, pt: (b * n_heads + h, 0, 0)

    def page_spec(t):
        return pl.BlockSpec((None, page_rows, HEAD_W),
                            lambda b, h, i, pt: (pt[b * n_heads + h, i * n_pp + t], 0, 0))

    grid_spec = pltpu.PrefetchScalarGridSpec(
        num_scalar_prefetch=1,
        grid=(batch, n_heads, half),
        in_specs=[pl.BlockSpec((1, QK_DIM), small)] * 4
                 + [pl.BlockSpec((1, HEAD_W), small),
                    pl.BlockSpec((tq, HEAD_W), lambda b, h, i, pt: (b * nq + i, h)),
                    pl.BlockSpec((tq, HEAD_W), lambda b, h, i, pt: (b * nq + nq - 1 - i, h)),
                    pl.BlockSpec((seq, HEAD_W), kv),
                    pl.BlockSpec((seq, HEAD_W), kv),
                    pl.BlockSpec((None, 2 * rows_q, HEAD_W), per_entry),
                    pl.BlockSpec((2 * rows_q, LANES), small),
                    pl.BlockSpec((2 * rows_q, LANES), small),
                    pl.BlockSpec((None, LANES, HEAD_W), per_entry),
                    pl.BlockSpec((None, LANES, HEAD_W), per_entry)]
                 + [page_spec(t) for t in range(n_pp)] * 2,
        out_specs=[pl.BlockSpec((2 * tq, HEAD_W), lambda b, h, i, pt: (b * half + i, h)),
                   pl.BlockSpec((None, rows_q, HEAD_W), per_entry)],
        scratch_shapes=[pltpu.VMEM((2 * tq, HEAD_W), BF16),
                        pltpu.VMEM((2 * tq, HEAD_W), BF16),
                        pltpu.VMEM((2 * tq, half * tq), F32),
                        pltpu.VMEM((2 * tq, seq), F32),
                        pltpu.VMEM((2 * rows_q, 1), F32),
                        pltpu.VMEM((2 * rows_q, 1), F32),
                        pltpu.VMEM((2 * rows_q, HEAD_W), F32)],
    )
    o_p, o_s = pl.pallas_call(
        functools.partial(_attn_kernel, tq=tq, nq=nq, n_pp=n_pp, lam_init=lam_init),
        grid_spec=grid_spec,
        out_shape=[jax.ShapeDtypeStruct((m, w), BF16),
                   jax.ShapeDtypeStruct((dec_batch, rows_q, HEAD_W), BF16)],
        compiler_params=_cparams(("arbitrary", "arbitrary", "arbitrary")),
        name="attention",
    )(page_table, *lams, subln_g, q_bf, q_bf, k_bf, v_bf, qt, bias, nbias, kn, vn,
      *([ck] * n_pp), *([cv] * n_pp))
    o_s = o_s.reshape(dec_batch, n_heads, dec_seq, HEAD_W).transpose(0, 2, 1, 3)
    return o_p, o_s.reshape(dec_batch * dec_seq, n_heads * HEAD_W)


def _ln_swish(y, g, b):
    mu = jnp.mean(y, axis=-1, keepdims=True)
    yc = y - mu
    var = jnp.mean(yc * yc, axis=-1, keepdims=True)
    z = yc * lax.rsqrt(var + EPS) * g + b
    return z * (1.0 / (1.0 + jnp.exp(-z)))


CONV_HALO = 32
CONV_CHUNK = 16


def _conv_block(t, cur_ref, tail_ref, w_ref, cb_ref, g_ref, b_ref, ext_ref, y_ref, tt):
    n_slab = cur_ref.shape[1] // LANES
    for sl in range(n_slab):
        lanes = slice(sl * LANES, (sl + 1) * LANES)
        tail = tail_ref[:, lanes]
        ext_ref[sl, 0:CONV_HALO, :] = jnp.where(t == 0, jnp.zeros_like(tail), tail)
        ext_ref[sl, CONV_HALO:CONV_HALO + tt, :] = cur_ref[:, lanes]
    off = CONV_HALO - (CONV_K - 1)
    n_ch = cur_ref.shape[1]

    half = CONV_CHUNK // 2
    for base in range(0, tt, CONV_CHUNK):
        for sl in range(n_slab):
            lanes = slice(sl * LANES, (sl + 1) * LANES)
            xs = [ext_ref[sl, pl.ds(base + off + k, half, stride=2), :] for k in range(CONV_K + 1)]
            even = jnp.zeros((half, LANES), F32)
            odd = jnp.zeros((half, LANES), F32)
            for j in range(CONV_K):
                w = w_ref[j:j + 1, lanes]
                even = even + xs[j] * w
                odd = odd + xs[j + 1] * w
            y_ref[sl, pl.ds(base, half, stride=2), :] = even + cb_ref[:, lanes]
            y_ref[sl, pl.ds(base + 1, half, stride=2), :] = odd + cb_ref[:, lanes]

    tot = y_ref[0]
    for sl in range(1, n_slab):
        tot = tot + y_ref[sl]
    mu = jnp.sum(tot, axis=-1, keepdims=True) * (1.0 / n_ch)
    sq = None
    for sl in range(n_slab):
        yc = y_ref[sl] - mu
        sq = yc * yc if sq is None else sq + yc * yc
    rstd = lax.rsqrt(jnp.sum(sq, axis=-1, keepdims=True) * (1.0 / n_ch) + EPS)
    out = []
    for sl in range(n_slab):
        lanes = slice(sl * LANES, (sl + 1) * LANES)
        z = (y_ref[sl] - mu) * rstd * g_ref[:, lanes] + b_ref[:, lanes]
        out.append(z * (1.0 / (1.0 + jnp.exp(-z))))
    return out


def _conv_sample_kernel(ext_ref, w_ref, cb_ref, g_ref, b_ref, o_ref):
    nb, rows, _ = o_ref.shape
    for e in range(nb):
        acc = jnp.zeros(o_ref.shape[1:], F32)
        for j in range(CONV_K):
            acc = acc + ext_ref[e, j:j + rows, :] * w_ref[j:j + 1, :]
        y = acc + cb_ref[...]
        o_ref[e] = _ln_swish(y, g_ref[...], b_ref[...]).astype(BF16)


CONV_SAMPLE_ROWS = 8
CONV_SAMPLE_ENTRIES = 8


def _conv_sample(ext, conv_w, conv_b, ln_g, ln_b, dec_seq):
    nb, n_rows, c = ext.shape
    assert dec_seq <= CONV_SAMPLE_ROWS and nb % CONV_SAMPLE_ENTRIES == 0
    pad_rows = (CONV_K - 1) + CONV_SAMPLE_ROWS
    ext_p = jnp.pad(ext, ((0, 0), (0, pad_rows - n_rows), (0, 0)))
    small = lambda b: (0, 0)
    o = pl.pallas_call(
        _conv_sample_kernel,
        grid=(nb // CONV_SAMPLE_ENTRIES,),
        in_specs=[pl.BlockSpec((CONV_SAMPLE_ENTRIES, pad_rows, c), lambda b: (b, 0, 0)),
                  pl.BlockSpec((CONV_K, c), small),
                  pl.BlockSpec((1, c), small),
                  pl.BlockSpec((1, c), small),
                  pl.BlockSpec((1, c), small)],
        out_specs=pl.BlockSpec((CONV_SAMPLE_ENTRIES, CONV_SAMPLE_ROWS, c), lambda b: (b, 0, 0)),
        out_shape=jax.ShapeDtypeStruct((nb, CONV_SAMPLE_ROWS, c), BF16),
        compiler_params=_cparams(("parallel",)),
        name="conv_sample",
    )(ext_p, conv_w, conv_b, ln_g, ln_b)
    return o[:, :dec_seq].reshape(nb * dec_seq, c)


def _outproj_kernel(o_ref, c_ref, x_ref, wa_ref, wc_ref, g_ref, h_ref, hn_ref):
    h = x_ref[...] + jnp.dot(o_ref[...], wa_ref[...], preferred_element_type=F32) \
        + jnp.dot(c_ref[...], wc_ref[...], preferred_element_type=F32)
    h_ref[...] = h
    ms = jnp.mean(h * h, axis=-1, keepdims=True)
    hn_ref[...] = (h * lax.rsqrt(ms + EPS) * g_ref[...]).astype(BF16)


def _outproj(o_n, c, x, w_out_bf, g2, tm):
    m, d = x.shape
    wa = o_n.shape[1]
    row = lambda i: (i, 0)
    return pl.pallas_call(
        _outproj_kernel,
        grid=(m // tm,),
        in_specs=[pl.BlockSpec((tm, wa), row),
                  pl.BlockSpec((tm, c.shape[1]), row),
                  pl.BlockSpec((tm, d), row),
                  pl.BlockSpec((wa, d), lambda i: (0, 0)),
                  pl.BlockSpec((c.shape[1], d), lambda i: (wa // c.shape[1], 0)),
                  pl.BlockSpec((1, d), lambda i: (0, 0))],
        out_specs=[pl.BlockSpec((tm, d), row), pl.BlockSpec((tm, d), row)],
        out_shape=[jax.ShapeDtypeStruct((m, d), F32), jax.ShapeDtypeStruct((m, d), BF16)],
        compiler_params=_cparams(("parallel",)),
        name="outproj",
    )(o_n, c, x, w_out_bf, w_out_bf, g2)


def _outproj_conv_kernel(o_ref, cur_ref, tail_ref, x_ref, wa_ref, wc_ref, cw_ref, cb_ref, lg_ref,
                         lb_ref, g_ref, h_ref, hn_ref, ext_ref, y_ref, *, tt):
    slabs = _conv_block(pl.program_id(1), cur_ref, tail_ref, cw_ref, cb_ref, lg_ref, lb_ref,
                        ext_ref, y_ref, tt)
    c = jnp.concatenate([s.astype(BF16) for s in slabs], axis=1)
    h = x_ref[...] + jnp.dot(o_ref[...], wa_ref[...], preferred_element_type=F32) \
        + jnp.dot(c, wc_ref[...], preferred_element_type=F32)
    h_ref[...] = h
    ms = jnp.mean(h * h, axis=-1, keepdims=True)
    hn_ref[...] = (h * lax.rsqrt(ms + EPS) * g_ref[...]).astype(BF16)


def _outproj_conv(o_n, u, x, w_out_bf, conv_w, conv_b, ln_g, ln_b, g2, batch, seq, tt, o_block):
    m, d = x.shape
    wa = o_n.shape[1]
    c = u.shape[1]
    nt = seq // tt
    ratio = tt // CONV_HALO
    row = lambda b, t: (b * nt + t, 0)
    tail = lambda b, t: (jnp.maximum((b * nt + t) * ratio - 1, 0), 0)
    small = lambda b, t: (0, 0)
    return pl.pallas_call(
        functools.partial(_outproj_conv_kernel, tt=tt),
        grid=(batch, nt),
        in_specs=[pl.BlockSpec((tt, wa), lambda b, t: (o_block(b * nt + t), 0)),
                  pl.BlockSpec((tt, c), row),
                  pl.BlockSpec((CONV_HALO, c), tail),
                  pl.BlockSpec((tt, d), row),
                  pl.BlockSpec((wa, d), small),
                  pl.BlockSpec((c, d), lambda b, t: (wa // c, 0)),
                  pl.BlockSpec((CONV_K, c), small),
                  pl.BlockSpec((1, c), small),
                  pl.BlockSpec((1, c), small),
                  pl.BlockSpec((1, c), small),
                  pl.BlockSpec((1, d), small)],
        out_specs=[pl.BlockSpec((tt, d), row), pl.BlockSpec((tt, d), row)],
        out_shape=[jax.ShapeDtypeStruct((m, d), F32), jax.ShapeDtypeStruct((m, d), BF16)],
        scratch_shapes=[pltpu.VMEM((c // LANES, CONV_HALO + tt, LANES), F32),
                        pltpu.VMEM((c // LANES, tt, LANES), F32)],
        compiler_params=_cparams(("parallel", "parallel")),
        name="outproj_conv",
    )(o_n, u, u, x, w_out_bf, w_out_bf, conv_w, conv_b, ln_g, ln_b, g2)


def _ffn_kernel(hn_ref, h_ref, wu_ref, wd_ref, g_ref, *refs, emit_w):
    if emit_w:
        wub_ref, wdb_ref, refs = refs[0], refs[1], refs[2:]
    y_ref, acc_ref = refs
    f = pl.program_id(1)

    @pl.when(f == 0)
    def _():
        acc_ref[...] = jnp.zeros(acc_ref.shape, F32)

    wu, wd = wu_ref[...], wd_ref[...]
    if emit_w:
        wu, wd = wu.astype(BF16), wd.astype(BF16)
        wub_ref[...] = wu
        wdb_ref[...] = wd
    a = jnp.maximum(jnp.dot(hn_ref[...], wu, preferred_element_type=F32), 0.0)
    acc_ref[...] += jnp.dot((a * a).astype(BF16), wd, preferred_element_type=F32)

    @pl.when(f == pl.num_programs(1) - 1)
    def _():
        y = h_ref[...] + acc_ref[...]
        ms = jnp.mean(y * y, axis=-1, keepdims=True)
        y_ref[...] = y * lax.rsqrt(ms + EPS) * g_ref[...]


def _ffn(hn, h, w_up, w_down, final_g, tm, tf):
    m, d = h.shape
    dff = w_up.shape[1]
    emit_w = w_up.dtype != BF16
    assert not emit_w or m == tm, "each weight tile must be visited exactly once to emit it"
    row = lambda i, f: (i, 0)
    up = lambda i, f: (0, f)
    down = lambda i, f: (f, 0)
    out_specs = [pl.BlockSpec((tm, d), row)]
    out_shape = [jax.ShapeDtypeStruct((m, d), F32)]
    if emit_w:
        out_specs = [pl.BlockSpec((d, tf), up), pl.BlockSpec((tf, d), down)] + out_specs
        out_shape = [jax.ShapeDtypeStruct(w_up.shape, BF16),
                     jax.ShapeDtypeStruct(w_down.shape, BF16)] + out_shape
    out = pl.pallas_call(
        functools.partial(_ffn_kernel, emit_w=emit_w),
        grid=(m // tm, dff // tf),
        in_specs=[pl.BlockSpec((tm, d), row),
                  pl.BlockSpec((tm, d), row),
                  pl.BlockSpec((d, tf), up),
                  pl.BlockSpec((tf, d), down),
                  pl.BlockSpec((1, d), lambda i, f: (0, 0))],
        out_specs=out_specs,
        out_shape=out_shape,
        scratch_shapes=[pltpu.VMEM((tm, d), F32)],
        compiler_params=_cparams(("parallel", "arbitrary")),
        name="ffn",
    )(hn, h, w_up, w_down, final_g)
    return out if emit_w else out[0]


def kernel(x_prompt, x_sample, cache_k, cache_v, state_conv, page_table, norm1_g, w_in,
           lambda_q1, lambda_k1, lambda_q2, lambda_k2, subln_g, conv_w, conv_b,
           conv_ln_g, conv_ln_b, w_out, norm2_g, w_up, w_down, final_g):
    depth = w_in.shape[0]
    assert depth == 1, "final norm is fused into the (single) layer's FFN kernel"
    batch, seq, d = x_prompt.shape
    dec_batch, dec_seq, _ = x_sample.shape
    n_heads, head_w = cache_k.shape[3], cache_k.shape[4]
    assert head_w == HEAD_W
    past = page_table.shape[1] * cache_k.shape[2]
    conv_c = conv_w.shape[2]

    xp = x_prompt.reshape(batch * seq, d)
    xs = x_sample.reshape(dec_batch * dec_seq, d)
    tm_s = dec_batch * dec_seq

    tabs_p = _rope_tables(np.arange(seq))
    tabs_s = _rope_tables(np.tile(past + np.arange(dec_seq), dec_batch))

    l = 0
    lam_init = 0.8 - 0.6 * math.exp(-0.3 * l)
    row = lambda a: a[l].reshape(1, -1)
    lams = (row(lambda_q1), row(lambda_k1), row(lambda_q2), row(lambda_k2))
    g1, g2, sg = row(norm1_g), row(norm2_g), row(subln_g)
    cb, lg, lb = row(conv_b), row(conv_ln_g), row(conv_ln_b)
    fg = final_g.reshape(1, -1)
    w_out_bf = w_out[l].astype(BF16)
    cw = conv_w[l]

    w_in_bf, q_s, k_s, kb_s, v_s, vb_s, u_s = _inproj(xs, g1, w_in[l], tabs_s, tm_s // 2, 2)
    q_p, k_p, kb_p, v_p, vb_p, u_p = _inproj(xp, g1, w_in_bf, tabs_p, TM_DENSE, seq // TM_DENSE)
    o_p, o_s = _attention(q_p, kb_p, vb_p, q_s, kb_s, vb_s, cache_k[l], cache_v[l], page_table,
                          lams, sg, batch, seq, dec_batch, dec_seq, lam_init, TQ)

    ext_s = jnp.concatenate([state_conv[l], u_s.reshape(dec_batch, dec_seq, conv_c)], axis=1)
    c_s = _conv_sample(ext_s, cw, cb, lg, lb, dec_seq)
    h_s, hn_s = _outproj(o_s, c_s, xs, w_out_bf, g2, tm_s)
    w_up_bf, w_down_bf, y_s = _ffn(hn_s, h_s, w_up[l], w_down[l], fg, tm_s, TF_SAMPLE)

    h_p, hn_p = _outproj_conv(o_p, u_p, xp, w_out_bf, cw, cb, lg, lb, g2, batch, seq, TQ,
                              functools.partial(_paired_block, nq=seq // TQ))
    y_p = _ffn(hn_p, h_p, w_up_bf, w_down_bf, fg, TM_DENSE, TF_PROMPT)

    keep = CONV_K - 1
    return (y_p.reshape(batch, seq, d),
            y_s.reshape(dec_batch, dec_seq, d),
            k_p.reshape(1, batch, seq, n_heads, head_w),
            v_p.reshape(1, batch, seq, n_heads, head_w),
            u_p.reshape(batch, seq, conv_c)[:, seq - keep:][None],
            k_s.reshape(1, dec_batch, dec_seq, n_heads, head_w),
            v_s.reshape(1, dec_batch, dec_seq, n_heads, head_w),
            ext_s[:, dec_seq:][None])
```

```python
import functools
import math

import numpy as np
import jax
import jax.numpy as jnp
from jax import lax
from jax.experimental import pallas as pl
from jax.experimental.pallas import tpu as pltpu

F32 = jnp.float32
BF16 = jnp.bfloat16

EPS = 1e-6
ROPE_THETA = 500000.0
LANES = 128
HEAD_W = 128
QK_DIM = 64
ROPE_DIM = 16
CONV_K = 31
NEG = -1e30
VMEM_LIMIT = 56 * 1024 * 1024

TM_DENSE = 512
TQ = 256
TF_PROMPT = 1024
TF_SAMPLE = 512


def _cparams(sem):
    return pltpu.CompilerParams(dimension_semantics=sem, vmem_limit_bytes=VMEM_LIMIT)


def _rope(z, c, s1, s2):
    return z * c + pltpu.roll(z, LANES - 8, 1) * s1 + pltpu.roll(z, 8, 1) * s2


def _inproj_kernel(x_ref, g_ref, w_ref, c_ref, s1_ref, s2_ref, *refs, emit_w):
    if emit_w:
        wb_ref, refs = refs[0], refs[1:]
    q_ref, k_ref, kb_ref, v_ref, vb_ref, u_ref, xn_ref, za_ref, zb_ref = refs
    j = pl.program_id(1)
    n_heads = za_ref.shape[1] // HEAD_W

    def matmul():
        w = w_ref[...]
        if emit_w:
            w = w.astype(BF16)
            wb_ref[...] = w
        return jnp.dot(xn_ref[...], w, preferred_element_type=F32)

    def roped(z_ref):
        c, s1, s2 = c_ref[...], s1_ref[...], s2_ref[...]
        for h in range(n_heads):
            sl = slice(h * HEAD_W, (h + 1) * HEAD_W)
            yield sl, _rope(z_ref[:, sl], c, s1, s2)

    @pl.when(j == 0)
    def _():
        x = x_ref[...]
        ms = jnp.mean(x * x, axis=-1, keepdims=True)
        xn_ref[...] = (x * lax.rsqrt(ms + EPS) * g_ref[...]).astype(BF16)
        za_ref[...] = matmul()

    @pl.when(j == 1)
    def _():
        zb_ref[...] = matmul()
        for sl, qr in roped(za_ref):
            q_ref[:, sl] = (qr * (QK_DIM ** -0.5)).astype(BF16)

    @pl.when(j == 2)
    def _():
        za_ref[...] = matmul()
        for sl, kr in roped(zb_ref):
            k_ref[:, sl] = kr
            kb_ref[:, sl] = kr.astype(BF16)

    @pl.when(j == 3)
    def _():
        zb_ref[...] = matmul()
        v = za_ref[...]
        v_ref[...] = v
        vb_ref[...] = v.astype(BF16)

    @pl.when(j == 4)
    def _():
        u_ref[...] = zb_ref[...] * (1.0 / (1.0 + jnp.exp(-matmul())))


def _inproj(x, g, w, tabs, tm, n_pos_blocks):
    m, d = x.shape
    nw = w.shape[1] // 5
    emit_w = w.dtype != BF16
    row = lambda i, j: (i, 0)
    tab = lambda i, j: (i % n_pos_blocks, 0)
    w_col = lambda i, j: (0, j)
    out_f = jax.ShapeDtypeStruct((m, nw), F32)
    out_b = jax.ShapeDtypeStruct((m, nw), BF16)
    out_specs = [pl.BlockSpec((tm, nw), row)] * 6
    out_shape = [out_b, out_f, out_b, out_f, out_b, out_f]
    if emit_w:
        out_specs = [pl.BlockSpec((d, nw), w_col)] + out_specs
        out_shape = [jax.ShapeDtypeStruct(w.shape, BF16)] + out_shape
    return pl.pallas_call(
        functools.partial(_inproj_kernel, emit_w=emit_w),
        grid=(m // tm, 5),
        in_specs=[pl.BlockSpec((tm, d), row),
                  pl.BlockSpec((1, d), lambda i, j: (0, 0)),
                  pl.BlockSpec((d, nw), w_col),
                  pl.BlockSpec((tm, LANES), tab),
                  pl.BlockSpec((tm, LANES), tab),
                  pl.BlockSpec((tm, LANES), tab)],
        out_specs=out_specs,
        out_shape=out_shape,
        scratch_shapes=[pltpu.VMEM((tm, d), BF16), pltpu.VMEM((tm, nw), F32),
                        pltpu.VMEM((tm, nw), F32)],
        compiler_params=_cparams(("parallel", "arbitrary")),
        name="inproj",
    )(x, g, w, *tabs)


def _rope_tables(pos):
    pos = np.asarray(pos, np.float64)
    half = ROPE_DIM // 2
    inv = np.power(ROPE_THETA, -np.arange(half, dtype=np.float64) * 2.0 / ROPE_DIM)
    ang = pos[:, None] * inv[None, :]
    cos, sin = np.cos(ang), np.sin(ang)
    t = pos.shape[0]
    one = np.ones((t, QK_DIM - ROPE_DIM))
    zero = np.zeros((t, QK_DIM - ROPE_DIM))
    zh = np.zeros((t, half))
    c = np.concatenate([cos, cos, one], axis=-1)
    s1 = np.concatenate([-sin, zh, zero], axis=-1)
    s2 = np.concatenate([zh, sin, zero], axis=-1)
    dup = lambda a: jnp.asarray(np.concatenate([a, a], axis=-1), F32)
    return dup(c), dup(s1), dup(s2)


def _lam(lq1, lk1, lq2, lk2, lam_init):
    a = jnp.sum(lq1[...] * lk1[...], axis=-1, keepdims=True)
    b = jnp.sum(lq2[...] * lk2[...], axis=-1, keepdims=True)
    return jnp.exp(a) - jnp.exp(b) + lam_init


def _subln(o, g, lam_init):
    ms = jnp.mean(o * o, axis=-1, keepdims=True)
    return (o * lax.rsqrt(ms + EPS) * g) * (1.0 - lam_init)


def _fold_lanes(x, op):
    f = x[:, 0:LANES]
    for t in range(1, x.shape[1] // LANES):
        f = op(f, x[:, t * LANES:(t + 1) * LANES])
    return f


ATTN_SEG = 3


def _attn_kernel(pt_ref, lq1, lk1, lq2, lk2, sg_ref, qlo_ref, qhi_ref, k_ref, v_ref,
                 qs_ref, bias_ref, nbias_ref, kn_ref, vn_ref, *rest, tq, nq, n_pp, lam_init):
    del pt_ref
    pk_refs = rest[:n_pp]
    pv_refs = rest[n_pp:2 * n_pp]
    o_ref, os_ref = rest[2 * n_pp:2 * n_pp + 2]
    qq_lo, qq_hi, s_lo, s_hi, m_ref, l_ref, acc_ref = rest[2 * n_pp + 2:]
    i = pl.program_id(2)
    lane = lax.broadcasted_iota(jnp.int32, (tq, HEAD_W), 1)
    for q_ref, qq_ref in ((qlo_ref, qq_lo), (qhi_ref, qq_hi)):
        q = q_ref[...].astype(F32)
        qq_ref[0:tq, :] = jnp.where(lane < QK_DIM, q, 0.0).astype(BF16)
        qq_ref[tq:2 * tq, :] = jnp.where(lane >= QK_DIM, q, 0.0).astype(BF16)

    def segment(qq_ref, s_ref, blocks, diag):
        mx = None
        for c in blocks:
            cols = slice(c * tq, (c + 1) * tq)
            s = lax.dot_general(qq_ref[...], k_ref[cols, :], (((1,), (1,)), ((), ())),
                                preferred_element_type=F32)
            if c == diag:
                r = lax.broadcasted_iota(jnp.int32, s.shape, 0)
                col = lax.broadcasted_iota(jnp.int32, s.shape, 1)
                r = jnp.where(r >= tq, r - tq, r)
                s = jnp.where(col <= r, s, NEG)
            s_ref[:, cols] = s
            f = _fold_lanes(s, jnp.maximum)
            mx = f if mx is None else jnp.maximum(mx, f)
        m = jnp.max(mx, axis=-1, keepdims=True)
        ls = None
        pv = None
        for c in blocks:
            cols = slice(c * tq, (c + 1) * tq)
            p = jnp.exp(s_ref[:, cols] - m)
            f = _fold_lanes(p, jnp.add)
            ls = f if ls is None else ls + f
            d = jnp.dot(p.astype(BF16), v_ref[cols, :], preferred_element_type=F32)
            pv = d if pv is None else pv + d
        return m, ls, pv

    def attend(qq_ref, s_ref, n_blk):
        parts = [segment(qq_ref, s_ref, range(c0, min(c0 + ATTN_SEG, n_blk)), n_blk - 1)
                 for c0 in range(0, n_blk, ATTN_SEG)]
        m = parts[0][0]
        for mp, _, _ in parts[1:]:
            m = jnp.maximum(m, mp)
        ls = None
        pv = None
        for mp, lsp, pvp in parts:
            w = jnp.exp(mp - m)
            ls = w * lsp if ls is None else ls + w * lsp
            pv = w * pvp if pv is None else pv + w * pvp
        accn = pv / jnp.sum(ls, axis=-1, keepdims=True)
        lam = _lam(lq1, lk1, lq2, lk2, lam_init)
        o = accn[0:tq, :] - lam * accn[tq:2 * tq, :]
        return _subln(o, sg_ref[...], lam_init).astype(BF16)

    def paged_update(kbs, vbs, bias):
        qs = qs_ref[...]
        ss = [lax.dot_general(qs, kb, (((1,), (1,)), ((), ())), preferred_element_type=F32) + bias
              for kb in kbs]
        mx = _fold_lanes(ss[0], jnp.maximum)
        for s in ss[1:]:
            mx = jnp.maximum(mx, _fold_lanes(s, jnp.maximum))
        m_prev = m_ref[...]
        m_new = jnp.maximum(m_prev, jnp.max(mx, axis=-1, keepdims=True))
        alpha = jnp.exp(m_prev - m_new)
        ls = None
        pv = None
        for s, vb in zip(ss, vbs):
            p = jnp.exp(s - m_new)
            f = _fold_lanes(p, jnp.add)
            ls = f if ls is None else ls + f
            d = jnp.dot(p.astype(BF16), vb, preferred_element_type=F32)
            pv = d if pv is None else pv + d
        l_ref[...] = alpha * l_ref[...] + jnp.sum(ls, axis=-1, keepdims=True)
        acc_ref[...] = alpha * acc_ref[...] + pv
        m_ref[...] = m_new

    def paged(step):
        if step == 0:
            m_ref[...] = jnp.full(m_ref.shape, NEG, F32)
            l_ref[...] = jnp.zeros(l_ref.shape, F32)
            acc_ref[...] = jnp.zeros(acc_ref.shape, F32)
            paged_update([kn_ref[...]], [vn_ref[...]], nbias_ref[...])
        bias = jnp.concatenate([bias_ref[...]] * (pk_refs[0].shape[0] // LANES), axis=1)
        paged_update([r[...].astype(BF16) for r in pk_refs],
                     [r[...].astype(BF16) for r in pv_refs], bias)
        if step == nq // 2 - 1:
            accn = acc_ref[...] / l_ref[...]
            half = accn.shape[0] // 2
            lam = _lam(lq1, lk1, lq2, lk2, lam_init)
            o = accn[0:half, :] - lam * accn[half:, :]
            os_ref[...] = _subln(o, sg_ref[...], lam_init).astype(BF16)

    def variant(step):
        paged(step)
        o_ref[0:tq, :] = attend(qq_lo, s_lo, step + 1)
        o_ref[tq:2 * tq, :] = attend(qq_hi, s_hi, nq - step)

    for n in range(nq // 2):
        pl.when(i == n)(functools.partial(variant, n))


def _paired_block(r, nq):
    b, rb = r // nq, r % nq
    return b * nq + jnp.where(rb < nq // 2, 2 * rb, 2 * (nq - 1 - rb) + 1)


def _attention(q_bf, k_bf, v_bf, q_s, kn_bf, vn_bf, cache_k, cache_v, page_table, lams, subln_g,
               batch, seq, dec_batch, dec_seq, lam_init, tq):
    m, w = q_bf.shape
    n_heads = w // HEAD_W
    nq = seq // tq
    half = nq // 2
    n_pool, page, _, _ = cache_k.shape
    n_pages = page_table.shape[1]
    n_pp = n_pages // half
    assert nq % 2 == 0 and n_pp * half == n_pages and batch * n_heads == dec_batch
    rows_q = dec_seq * n_heads
    page_rows = page * n_heads
    q4 = q_s.reshape(dec_batch, dec_seq, n_heads, HEAD_W).transpose(0, 2, 1, 3)
    q4 = q4.reshape(dec_batch, rows_q, HEAD_W)
    lane = jnp.arange(HEAD_W)
    qt = jnp.concatenate([jnp.where(lane < QK_DIM, q4, 0), jnp.where(lane >= QK_DIM, q4, 0)],
                         axis=1).astype(BF16)
    pad = lambda a: jnp.pad(a.reshape(dec_batch, rows_q, HEAD_W),
                            ((0, 0), (0, LANES - rows_q), (0, 0)))
    kn, vn = pad(kn_bf), pad(vn_bf)
    c = np.arange(2 * rows_q)
    c_head = (c % rows_q) // dec_seq
    c_query = c % dec_seq
    assert LANES % n_heads == 0
    r = np.arange(LANES)
    bias = jnp.asarray(np.where((r % n_heads)[None, :] == c_head[:, None], 0.0, NEG), F32)
    rn = np.arange(LANES)
    ok = ((rn % n_heads)[None, :] == c_head[:, None]) & ((rn // n_heads)[None, :] <= c_query[:, None]) \
        & (rn < rows_q)[None, :]
    nbias = jnp.asarray(np.where(ok, 0.0, NEG), F32)
    ck = cache_k.reshape(n_pool, page_rows, HEAD_W)
    cv = cache_v.reshape(n_pool, page_rows, HEAD_W)

    small = lambda b, h, i, pt: (0, 0)
    kv = lambda b, h, i, pt: (b, h)
    per_entry = lambda b, h, i, pt: (b * n_heads + h, 0, 0)

    def page_spec(t):
        return pl.BlockSpec((None, page_rows, HEAD_W),
                            lambda b, h, i, pt: (pt[b * n_heads + h, i * n_pp + t], 0, 0))

    grid_spec = pltpu.PrefetchScalarGridSpec(
        num_scalar_prefetch=1,
        grid=(batch, n_heads, half),
        in_specs=[pl.BlockSpec((1, QK_DIM), small)] * 4
                 + [pl.BlockSpec((1, HEAD_W), small),
                    pl.BlockSpec((tq, HEAD_W), lambda b, h, i, pt: (b * nq + i, h)),
                    pl.BlockSpec((tq, HEAD_W), lambda b, h, i, pt: (b * nq + nq - 1 - i, h)),
                    pl.BlockSpec((seq, HEAD_W), kv),
                    pl.BlockSpec((seq, HEAD_W), kv),
                    pl.BlockSpec((None, 2 * rows_q, HEAD_W), per_entry),
                    pl.BlockSpec((2 * rows_q, LANES), small),
                    pl.BlockSpec((2 * rows_q, LANES), small),
                    pl.BlockSpec((None, LANES, HEAD_W), per_entry),
                    pl.BlockSpec((None, LANES, HEAD_W), per_entry)]
                 + [page_spec(t) for t in range(n_pp)] * 2,
        out_specs=[pl.BlockSpec((2 * tq, HEAD_W), lambda b, h, i, pt: (b * half + i, h)),
                   pl.BlockSpec((None, rows_q, HEAD_W), per_entry)],
        scratch_shapes=[pltpu.VMEM((2 * tq, HEAD_W), BF16),
                        pltpu.VMEM((2 * tq, HEAD_W), BF16),
                        pltpu.VMEM((2 * tq, half * tq), F32),
                        pltpu.VMEM((2 * tq, seq), F32),
                        pltpu.VMEM((2 * rows_q, 1), F32),
                        pltpu.VMEM((2 * rows_q, 1), F32),
                        pltpu.VMEM((2 * rows_q, HEAD_W), F32)],
    )
    o_p, o_s = pl.pallas_call(
        functools.partial(_attn_kernel, tq=tq, nq=nq, n_pp=n_pp, lam_init=lam_init),
        grid_spec=grid_spec,
        out_shape=[jax.ShapeDtypeStruct((m, w), BF16),
                   jax.ShapeDtypeStruct((dec_batch, rows_q, HEAD_W), BF16)],
        compiler_params=_cparams(("arbitrary", "arbitrary", "arbitrary")),
        name="attention",
    )(page_table, *lams, subln_g, q_bf, q_bf, k_bf, v_bf, qt, bias, nbias, kn, vn,
      *([ck] * n_pp), *([cv] * n_pp))
    o_s = o_s.reshape(dec_batch, n_heads, dec_seq, HEAD_W).transpose(0, 2, 1, 3)
    return o_p, o_s.reshape(dec_batch * dec_seq, n_heads * HEAD_W)


def _ln_swish(y, g, b):
    mu = jnp.mean(y, axis=-1, keepdims=True)
    yc = y - mu
    var = jnp.mean(yc * yc, axis=-1, keepdims=True)
    z = yc * lax.rsqrt(var + EPS) * g + b
    return z * (1.0 / (1.0 + jnp.exp(-z)))


CONV_HALO = 32
CONV_CHUNK = 16


def _conv_block(t, cur_ref, tail_ref, w_ref, cb_ref, g_ref, b_ref, ext_ref, y_ref, tt):
    n_slab = cur_ref.shape[1] // LANES
    for sl in range(n_slab):
        lanes = slice(sl * LANES, (sl + 1) * LANES)
        tail = tail_ref[:, lanes]
        ext_ref[sl, 0:CONV_HALO, :] = jnp.where(t == 0, jnp.zeros_like(tail), tail)
        ext_ref[sl, CONV_HALO:CONV_HALO + tt, :] = cur_ref[:, lanes]
    off = CONV_HALO - (CONV_K - 1)
    n_ch = cur_ref.shape[1]

    half = CONV_CHUNK // 2
    for base in range(0, tt, CONV_CHUNK):
        for sl in range(n_slab):
            lanes = slice(sl * LANES, (sl + 1) * LANES)
            xs = [ext_ref[sl, pl.ds(base + off + k, half, stride=2), :] for k in range(CONV_K + 1)]
            even = jnp.zeros((half, LANES), F32)
            odd = jnp.zeros((half, LANES), F32)
            for j in range(CONV_K):
                w = w_ref[j:j + 1, lanes]
                even = even + xs[j] * w
                odd = odd + xs[j + 1] * w
            y_ref[sl, pl.ds(base, half, stride=2), :] = even + cb_ref[:, lanes]
            y_ref[sl, pl.ds(base + 1, half, stride=2), :] = odd + cb_ref[:, lanes]

    tot = y_ref[0]
    for sl in range(1, n_slab):
        tot = tot + y_ref[sl]
    mu = jnp.sum(tot, axis=-1, keepdims=True) * (1.0 / n_ch)
    sq = None
    for sl in range(n_slab):
        yc = y_ref[sl] - mu
        sq = yc * yc if sq is None else sq + yc * yc
    rstd = lax.rsqrt(jnp.sum(sq, axis=-1, keepdims=True) * (1.0 / n_ch) + EPS)
    out = []
    for sl in range(n_slab):
        lanes = slice(sl * LANES, (sl + 1) * LANES)
        z = (y_ref[sl] - mu) * rstd * g_ref[:, lanes] + b_ref[:, lanes]
        out.append(z * (1.0 / (1.0 + jnp.exp(-z))))
    return out


def _conv_sample_kernel(ext_ref, w_ref, cb_ref, g_ref, b_ref, o_ref):
    nb, rows, _ = o_ref.shape
    for e in range(nb):
        acc = jnp.zeros(o_ref.shape[1:], F32)
        for j in range(CONV_K):
            acc = acc + ext_ref[e, j:j + rows, :] * w_ref[j:j + 1, :]
        y = acc + cb_ref[...]
        o_ref[e] = _ln_swish(y, g_ref[...], b_ref[...]).astype(BF16)


CONV_SAMPLE_ROWS = 8
CONV_SAMPLE_ENTRIES = 8


def _conv_sample(ext, conv_w, conv_b, ln_g, ln_b, dec_seq):
    nb, n_rows, c = ext.shape
    assert dec_seq <= CONV_SAMPLE_ROWS and nb % CONV_SAMPLE_ENTRIES == 0
    pad_rows = (CONV_K - 1) + CONV_SAMPLE_ROWS
    ext_p = jnp.pad(ext, ((0, 0), (0, pad_rows - n_rows), (0, 0)))
    small = lambda b: (0, 0)
    o = pl.pallas_call(
        _conv_sample_kernel,
        grid=(nb // CONV_SAMPLE_ENTRIES,),
        in_specs=[pl.BlockSpec((CONV_SAMPLE_ENTRIES, pad_rows, c), lambda b: (b, 0, 0)),
                  pl.BlockSpec((CONV_K, c), small),
                  pl.BlockSpec((1, c), small),
                  pl.BlockSpec((1, c), small),
                  pl.BlockSpec((1, c), small)],
        out_specs=pl.BlockSpec((CONV_SAMPLE_ENTRIES, CONV_SAMPLE_ROWS, c), lambda b: (b, 0, 0)),
        out_shape=jax.ShapeDtypeStruct((nb, CONV_SAMPLE_ROWS, c), BF16),
        compiler_params=_cparams(("parallel",)),
        name="conv_sample",
    )(ext_p, conv_w, conv_b, ln_g, ln_b)
    return o[:, :dec_seq].reshape(nb * dec_seq, c)


def _outproj_kernel(o_ref, c_ref, x_ref, wa_ref, wc_ref, g_ref, h_ref, hn_ref):
    h = x_ref[...] + jnp.dot(o_ref[...], wa_ref[...], preferred_element_type=F32) \
        + jnp.dot(c_ref[...], wc_ref[...], preferred_element_type=F32)
    h_ref[...] = h
    ms = jnp.mean(h * h, axis=-1, keepdims=True)
    hn_ref[...] = (h * lax.rsqrt(ms + EPS) * g_ref[...]).astype(BF16)


def _outproj(o_n, c, x, w_out_bf, g2, tm):
    m, d = x.shape
    wa = o_n.shape[1]
    row = lambda i: (i, 0)
    return pl.pallas_call(
        _outproj_kernel,
        grid=(m // tm,),
        in_specs=[pl.BlockSpec((tm, wa), row),
                  pl.BlockSpec((tm, c.shape[1]), row),
                  pl.BlockSpec((tm, d), row),
                  pl.BlockSpec((wa, d), lambda i: (0, 0)),
                  pl.BlockSpec((c.shape[1], d), lambda i: (wa // c.shape[1], 0)),
                  pl.BlockSpec((1, d), lambda i: (0, 0))],
        out_specs=[pl.BlockSpec((tm, d), row), pl.BlockSpec((tm, d), row)],
        out_shape=[jax.ShapeDtypeStruct((m, d), F32), jax.ShapeDtypeStruct((m, d), BF16)],
        compiler_params=_cparams(("parallel",)),
        name="outproj",
    )(o_n, c, x, w_out_bf, w_out_bf, g2)


def _outproj_conv_kernel(o_ref, cur_ref, tail_ref, x_ref, wa_ref, wc_ref, cw_ref, cb_ref, lg_ref,
                         lb_ref, g_ref, h_ref, hn_ref, ext_ref, y_ref, *, tt):
    slabs = _conv_block(pl.program_id(1), cur_ref, tail_ref, cw_ref, cb_ref, lg_ref, lb_ref,
                        ext_ref, y_ref, tt)
    c = jnp.concatenate([s.astype(BF16) for s in slabs], axis=1)
    h = x_ref[...] + jnp.dot(o_ref[...], wa_ref[...], preferred_element_type=F32) \
        + jnp.dot(c, wc_ref[...], preferred_element_type=F32)
    h_ref[...] = h
    ms = jnp.mean(h * h, axis=-1, keepdims=True)
    hn_ref[...] = (h * lax.rsqrt(ms + EPS) * g_ref[...]).astype(BF16)


def _outproj_conv(o_n, u, x, w_out_bf, conv_w, conv_b, ln_g, ln_b, g2, batch, seq, tt, o_block):
    m, d = x.shape
    wa = o_n.shape[1]
    c = u.shape[1]
    nt = seq // tt
    ratio = tt // CONV_HALO
    row = lambda b, t: (b * nt + t, 0)
    tail = lambda b, t: (jnp.maximum((b * nt + t) * ratio - 1, 0), 0)
    small = lambda b, t: (0, 0)
    return pl.pallas_call(
        functools.partial(_outproj_conv_kernel, tt=tt),
        grid=(batch, nt),
        in_specs=[pl.BlockSpec((tt, wa), lambda b, t: (o_block(b * nt + t), 0)),
                  pl.BlockSpec((tt, c), row),
                  pl.BlockSpec((CONV_HALO, c), tail),
                  pl.BlockSpec((tt, d), row),
                  pl.BlockSpec((wa, d), small),
                  pl.BlockSpec((c, d), lambda b, t: (wa // c, 0)),
                  pl.BlockSpec((CONV_K, c), small),
                  pl.BlockSpec((1, c), small),
                  pl.BlockSpec((1, c), small),
                  pl.BlockSpec((1, c), small),
                  pl.BlockSpec((1, d), small)],
        out_specs=[pl.BlockSpec((tt, d), row), pl.BlockSpec((tt, d), row)],
        out_shape=[jax.ShapeDtypeStruct((m, d), F32), jax.ShapeDtypeStruct((m, d), BF16)],
        scratch_shapes=[pltpu.VMEM((c // LANES, CONV_HALO + tt, LANES), F32),
                        pltpu.VMEM((c // LANES, tt, LANES), F32)],
        compiler_params=_cparams(("parallel", "parallel")),
        name="outproj_conv",
    )(o_n, u, u, x, w_out_bf, w_out_bf, conv_w, conv_b, ln_g, ln_b, g2)


def _ffn_kernel(hn_ref, h_ref, wu_ref, wd_ref, g_ref, *refs, emit_w):
    if emit_w:
        wub_ref, wdb_ref, refs = refs[0], refs[1], refs[2:]
    y_ref, acc_ref = refs
    f = pl.program_id(1)
    last = pl.num_programs(1) - 1

    def chunk():
        wu, wd = wu_ref[...], wd_ref[...]
        if emit_w:
            wu, wd = wu.astype(BF16), wd.astype(BF16)
            wub_ref[...] = wu
            wdb_ref[...] = wd
        a = jnp.maximum(jnp.dot(hn_ref[...], wu, preferred_element_type=F32), 0.0)
        return jnp.dot((a * a).astype(BF16), wd, preferred_element_type=F32)

    @pl.when(f == 0)
    def _():
        acc_ref[...] = chunk()

    @pl.when((f > 0) & (f < last))
    def _():
        acc_ref[...] += chunk()

    @pl.when(f == last)
    def _():
        y = h_ref[...] + acc_ref[...] + chunk()
        ms = jnp.mean(y * y, axis=-1, keepdims=True)
        y_ref[...] = y * lax.rsqrt(ms + EPS) * g_ref[...]


def _ffn(hn, h, w_up, w_down, final_g, tm, tf):
    m, d = h.shape
    dff = w_up.shape[1]
    emit_w = w_up.dtype != BF16
    assert not emit_w or m == tm, "each weight tile must be visited exactly once to emit it"
    assert dff // tf >= 2, "first and last hidden chunk must be different grid steps"
    row = lambda i, f: (i, 0)
    up = lambda i, f: (0, f)
    down = lambda i, f: (f, 0)
    out_specs = [pl.BlockSpec((tm, d), row)]
    out_shape = [jax.ShapeDtypeStruct((m, d), F32)]
    if emit_w:
        out_specs = [pl.BlockSpec((d, tf), up), pl.BlockSpec((tf, d), down)] + out_specs
        out_shape = [jax.ShapeDtypeStruct(w_up.shape, BF16),
                     jax.ShapeDtypeStruct(w_down.shape, BF16)] + out_shape
    out = pl.pallas_call(
        functools.partial(_ffn_kernel, emit_w=emit_w),
        grid=(m // tm, dff // tf),
        in_specs=[pl.BlockSpec((tm, d), row),
                  pl.BlockSpec((tm, d), row),
                  pl.BlockSpec((d, tf), up),
                  pl.BlockSpec((tf, d), down),
                  pl.BlockSpec((1, d), lambda i, f: (0, 0))],
        out_specs=out_specs,
        out_shape=out_shape,
        scratch_shapes=[pltpu.VMEM((tm, d), F32)],
        compiler_params=_cparams(("parallel", "arbitrary")),
        name="ffn",
    )(hn, h, w_up, w_down, final_g)
    return out if emit_w else out[0]


def kernel(x_prompt, x_sample, cache_k, cache_v, state_conv, page_table, norm1_g, w_in,
           lambda_q1, lambda_k1, lambda_q2, lambda_k2, subln_g, conv_w, conv_b,
           conv_ln_g, conv_ln_b, w_out, norm2_g, w_up, w_down, final_g):
    depth = w_in.shape[0]
    assert depth == 1, "final norm is fused into the (single) layer's FFN kernel"
    batch, seq, d = x_prompt.shape
    dec_batch, dec_seq, _ = x_sample.shape
    n_heads, head_w = cache_k.shape[3], cache_k.shape[4]
    assert head_w == HEAD_W
    past = page_table.shape[1] * cache_k.shape[2]
    conv_c = conv_w.shape[2]

    xp = x_prompt.reshape(batch * seq, d)
    xs = x_sample.reshape(dec_batch * dec_seq, d)
    tm_s = dec_batch * dec_seq

    tabs_p = _rope_tables(np.arange(seq))
    tabs_s = _rope_tables(np.tile(past + np.arange(dec_seq), dec_batch))

    l = 0
    lam_init = 0.8 - 0.6 * math.exp(-0.3 * l)
    row = lambda a: a[l].reshape(1, -1)
    lams = (row(lambda_q1), row(lambda_k1), row(lambda_q2), row(lambda_k2))
    g1, g2, sg = row(norm1_g), row(norm2_g), row(subln_g)
    cb, lg, lb = row(conv_b), row(conv_ln_g), row(conv_ln_b)
    fg = final_g.reshape(1, -1)
    w_out_bf = w_out[l].astype(BF16)
    cw = conv_w[l]

    w_in_bf, q_s, k_s, kb_s, v_s, vb_s, u_s = _inproj(xs, g1, w_in[l], tabs_s, tm_s, 1)
    q_p, k_p, kb_p, v_p, vb_p, u_p = _inproj(xp, g1, w_in_bf, tabs_p, TM_DENSE, seq // TM_DENSE)
    o_p, o_s = _attention(q_p, kb_p, vb_p, q_s, kb_s, vb_s, cache_k[l], cache_v[l], page_table,
                          lams, sg, batch, seq, dec_batch, dec_seq, lam_init, TQ)

    ext_s = jnp.concatenate([state_conv[l], u_s.reshape(dec_batch, dec_seq, conv_c)], axis=1)
    c_s = _conv_sample(ext_s, cw, cb, lg, lb, dec_seq)
    h_s, hn_s = _outproj(o_s, c_s, xs, w_out_bf, g2, tm_s)
    w_up_bf, w_down_bf, y_s = _ffn(hn_s, h_s, w_up[l], w_down[l], fg, tm_s, TF_SAMPLE)

    h_p, hn_p = _outproj_conv(o_p, u_p, xp, w_out_bf, cw, cb, lg, lb, g2, batch, seq, TQ,
                              functools.partial(_paired_block, nq=seq // TQ))
    y_p = _ffn(hn_p, h_p, w_up_bf, w_down_bf, fg, TM_DENSE, TF_PROMPT)

    keep = CONV_K - 1
    return (y_p.reshape(batch, seq, d),
            y_s.reshape(dec_batch, dec_seq, d),
            k_p.reshape(1, batch, seq, n_heads, head_w),
            v_p.reshape(1, batch, seq, n_heads, head_w),
            u_p.reshape(batch, seq, conv_c)[:, seq - keep:][None],
            k_s.reshape(1, dec_batch, dec_seq, n_heads, head_w),
            v_s.reshape(1, dec_batch, dec_seq, n_heads, head_w),
            ext_s[:, dec_seq:][None])
```
